```python
import math
import jax
import jax.numpy as jnp
from jax import lax
import numpy as np

D_MODEL = 1024
BATCH = 16
SEQ = 2048
DEPTH = 2

HEAD_DIM = 64
ATTN_HEADS = 8
DILATED_GROUPS = ((128, 1), (512, 4), (2048, 16))
N_GROUPS = len(DILATED_GROUPS)
ATTN_WIDTH = ATTN_HEADS * HEAD_DIM
ROPE_THETA = 10000.0
NEG_INF = -1e30
HYENA_WIDTH = D_MODEL // 2
SHORT_CONV = 3
FILTER_BANDS = 16
FILTER_EMB_DIM = 1 + 2 * FILTER_BANDS
FILTER_HIDDEN = 64
FILTER_INNER = 2
DECAY_TARGET = 1e-2
FAST_DECAY_PCT = 0.3
SLOW_DECAY_PCT = 1.5
N_BRANCHES = 2
HY_IN_WIDTH = 3 * HYENA_WIDTH
QKV_WIDTH = N_GROUPS * 3 * ATTN_WIDTH
IN_WIDTH = HY_IN_WIDTH + QKV_WIDTH + N_BRANCHES * D_MODEL
D_FF = 4 * D_MODEL
RMS_EPS = 1e-6

kernel_name = "hyena_dilated_attn_gated_hybrid"


def rmsnorm(x, gain):
    xf = x.astype(jnp.float32)
    y = xf * lax.rsqrt(jnp.mean(jnp.square(xf), axis=-1, keepdims=True) + RMS_EPS)
    return (y * gain.astype(jnp.float32)).astype(x.dtype)


def rotary(t, positions):
    half = t.shape[-1] // 2
    inv_freq = ROPE_THETA ** (-jnp.arange(half, dtype=jnp.float32) / half)
    ang = positions.astype(jnp.float32)[:, None] * inv_freq[None, :]
    cos = jnp.cos(ang)[None, :, None, :]
    sin = jnp.sin(ang)[None, :, None, :]
    tf = t.astype(jnp.float32)
    t1, t2 = tf[..., :half], tf[..., half:]
    return jnp.concatenate([t1 * cos - t2 * sin, t2 * cos + t1 * sin], axis=-1).astype(t.dtype)


def short_conv_centred(u, w, b):
    k_width = w.shape[0]
    s = u.shape[1]
    pad = k_width // 2
    up = jnp.pad(u, ((0, 0), (pad, k_width - 1 - pad), (0, 0)))
    out = b
    for tap in range(k_width):
        out = out + up[:, tap:tap + s] * w[tap]
    return out


def hyena_filters(length, w1, b1, w_inner, b_inner, w_out, freq):
    f32 = lambda a: a.astype(jnp.float32)
    n = jnp.arange(length, dtype=jnp.float32)
    t = n / max(length - 1, 1)
    bands = jnp.linspace(1e-4, FILTER_BANDS - 1, FILTER_BANDS, dtype=jnp.float32)
    ang = (2.0 * math.pi / length) * n[:, None] * bands[None, :]
    z = jnp.concatenate([t[:, None], jnp.cos(ang), -jnp.sin(ang)], axis=-1)
    fr = f32(freq)
    hid = jnp.sin(fr * (z @ f32(w1) + f32(b1)))
    for i in range(FILTER_INNER):
        hid = jnp.sin(fr * (hid @ f32(w_inner[i]) + f32(b_inner[i])))
    filt = (hid @ f32(w_out)).reshape(length, 2, HYENA_WIDTH)
    max_decay = math.log(DECAY_TARGET) / FAST_DECAY_PCT
    min_decay = math.log(DECAY_TARGET) / SLOW_DECAY_PCT
    deltas = jnp.abs(jnp.linspace(min_decay, max_decay, HYENA_WIDTH, dtype=jnp.float32))
    window = jnp.exp(-t[:, None] * deltas[None, :])
    filt = filt * window[:, None, :]
    return filt[:, 0], filt[:, 1]


def bidir_fftconv(u, h_fwd, h_bwd, d_skip):
    length, chans = u.shape[1], u.shape[2]
    n_fft = 2 * length
    kern = jnp.concatenate([h_fwd, jnp.zeros((1, chans), jnp.float32), h_bwd[:0:-1]], axis=0)
    uf = u.astype(jnp.float32)
    u_f = jnp.fft.rfft(uf, n=n_fft, axis=1)
    k_f = jnp.fft.rfft(kern, n=n_fft, axis=0)
    y = jnp.fft.irfft(u_f * k_f[None], n=n_fft, axis=1)[:, :length]
    return (y + uf * d_skip.astype(jnp.float32)).astype(u.dtype)


def dilated_window_attention(q, k, v, dilation, radius):
    b, s, h, e = q.shape
    n = s // dilation
    blk = radius
    nb = -(-n // blk)
    n_pad = nb * blk

    def to_sub(t):
        return t.reshape(b, n, dilation, h, e).transpose(0, 2, 1, 3, 4)

    def band(t):
        tp = jnp.pad(t, ((0, 0), (0, 0), (blk, n_pad - n + blk), (0, 0), (0, 0)))
        tp = tp.reshape(b, dilation, nb + 2, blk, h, e)
        return jnp.concatenate([tp[:, :, :-2], tp[:, :, 1:-1], tp[:, :, 2:]], axis=3)

    qs = jnp.pad(to_sub(q) * (HEAD_DIM ** -0.5), ((0, 0), (0, 0), (0, n_pad - n), (0, 0), (0, 0)))
    qb = qs.reshape(b, dilation, nb, blk, h, e)
    kb = band(to_sub(k))
    vb = band(to_sub(v))
    q_idx = jnp.arange(nb)[:, None] * blk + jnp.arange(blk)[None, :]
    k_idx = (jnp.arange(nb)[:, None] - 1) * blk + jnp.arange(3 * blk)[None, :]
    rel = k_idx[:, None, :] - q_idx[:, :, None]
    valid = (jnp.abs(rel) <= radius) & (k_idx[:, None, :] >= 0) & (k_idx[:, None, :] < n)
    scores = jnp.einsum('bdnqhe,bdnkhe->bdnhqk', qb, kb).astype(jnp.float32)
    scores = jnp.where(valid[None, None, :, None], scores, NEG_INF)
    lse = jax.nn.logsumexp(scores, axis=-1)
    probs = jnp.exp(scores - lse[..., None]).astype(v.dtype)
    out = jnp.einsum('bdnhqk,bdnkhe->bdnqhe', probs, vb)
    out = out.reshape(b, dilation, n_pad, h, e)[:, :, :n].transpose(0, 2, 1, 3, 4).reshape(b, s, h, e)
    lse = lse.transpose(0, 1, 2, 4, 3).reshape(b, dilation, n_pad, h)[:, :, :n]
    lse = lse.transpose(0, 2, 1, 3).reshape(b, s, h)
    return out, lse


def hybrid_mixer(hn, w_in, conv_w, conv_b, filt_w1, filt_b1, filt_w_inner, filt_b_inner,
                 filt_w_out, filt_freq, hy_skip, p_hy, p_att, w_o):
    b, s, _ = hn.shape
    proj = hn @ w_in
    hy_in = proj[..., :HY_IN_WIDTH]
    qkv = proj[..., HY_IN_WIDTH:HY_IN_WIDTH + QKV_WIDTH]
    gate_logits = proj[..., HY_IN_WIDTH + QKV_WIDTH:]

    u = short_conv_centred(hy_in, conv_w, conv_b)
    x0, x1, hv = jnp.split(u, 3, axis=-1)
    h_fwd, h_bwd = hyena_filters(s, filt_w1, filt_b1, filt_w_inner, filt_b_inner, filt_w_out, filt_freq)
    y_hy = x0 * bidir_fftconv(x1 * hv, h_fwd, h_bwd, hy_skip)

    qkv = qkv.reshape(b, s, N_GROUPS, 3, ATTN_HEADS, HEAD_DIM)
    positions = jnp.arange(s)
    outs, lses = [], []
    for g, (window, dilation) in enumerate(DILATED_GROUPS):
        q = rotary(qkv[:, :, g, 0], positions)
        k = rotary(qkv[:, :, g, 1], positions)
        o_g, lse_g = dilated_window_attention(q, k, qkv[:, :, g, 2], dilation, window // (2 * dilation))
        outs.append(o_g)
        lses.append(lse_g)
    group_w = jax.nn.softmax(jnp.stack(lses, axis=0), axis=0).astype(hn.dtype)
    y_att = jnp.einsum('gbsh,gbshe->bshe', group_w, jnp.stack(outs, axis=0)).reshape(b, s, ATTN_WIDTH)

    gates = jax.nn.sigmoid(gate_logits)
    g_hy, g_att = gates[..., :D_MODEL], gates[..., D_MODEL:]
    merged = g_hy * (y_hy @ p_hy) + g_att * (y_att @ p_att)
    return merged @ w_o


def setup_inputs(seed: int = 0) -> dict:
    key = jax.random.key(seed)
    ks = jax.random.split(key, 20)

    def normal(k, shape, scale):
        return jax.random.normal(k, shape, jnp.float32) * scale

    return {
        "x": normal(ks[0], (BATCH, SEQ, D_MODEL), 1.0),
        "norm_mix": 1.0 + normal(ks[1], (DEPTH, D_MODEL), 0.02),
        "w_in": normal(ks[2], (DEPTH, D_MODEL, IN_WIDTH), D_MODEL ** -0.5),
        "conv_w": normal(ks[3], (DEPTH, SHORT_CONV, HY_IN_WIDTH), SHORT_CONV ** -0.5),
        "conv_b": normal(ks[4], (DEPTH, HY_IN_WIDTH), 0.02),
        "filt_w1": normal(ks[5], (DEPTH, FILTER_EMB_DIM, FILTER_HIDDEN), FILTER_EMB_DIM ** -0.5),
        "filt_b1": normal(ks[6], (DEPTH, FILTER_HIDDEN), 0.02),
        "filt_w_inner": normal(ks[7], (DEPTH, FILTER_INNER, FILTER_HIDDEN, FILTER_HIDDEN), FILTER_HIDDEN ** -0.5),
        "filt_b_inner": normal(ks[8], (DEPTH, FILTER_INNER, FILTER_HIDDEN), 0.02),
        "filt_w_out": normal(ks[9], (DEPTH, FILTER_HIDDEN, 2 * HYENA_WIDTH), 0.1 * FILTER_HIDDEN ** -0.5),
        "filt_freq": 1.0 + normal(ks[10], (DEPTH, FILTER_HIDDEN), 0.02),
        "hy_skip": normal(ks[11], (DEPTH, HYENA_WIDTH), 0.5),
        "p_hy": normal(ks[12], (DEPTH, HYENA_WIDTH, D_MODEL), HYENA_WIDTH ** -0.5),
        "p_att": normal(ks[13], (DEPTH, ATTN_WIDTH, D_MODEL), ATTN_WIDTH ** -0.5),
        "w_o": normal(ks[14], (DEPTH, D_MODEL, D_MODEL), D_MODEL ** -0.5),
        "norm_ffn": 1.0 + normal(ks[15], (DEPTH, D_MODEL), 0.02),
        "w_ff1": normal(ks[16], (DEPTH, D_MODEL, D_FF), D_MODEL ** -0.5),
        "w_ff2": normal(ks[17], (DEPTH, D_FF, D_MODEL), D_FF ** -0.5),
        "norm_final": 1.0 + normal(ks[18], (D_MODEL,), 0.02),
    }


def reference(x, norm_mix, w_in, conv_w, conv_b, filt_w1, filt_b1, filt_w_inner, filt_b_inner,
              filt_w_out, filt_freq, hy_skip, p_hy, p_att, w_o, norm_ffn, w_ff1, w_ff2, norm_final):
    for layer in range(DEPTH):
        hn = rmsnorm(x, norm_mix[layer])
        x = x + hybrid_mixer(hn, w_in[layer], conv_w[layer], conv_b[layer], filt_w1[layer], filt_b1[layer],
                             filt_w_inner[layer], filt_b_inner[layer], filt_w_out[layer], filt_freq[layer],
                             hy_skip[layer], p_hy[layer], p_att[layer], w_o[layer])
        hn = rmsnorm(x, norm_ffn[layer])
        x = x + jnp.square(jax.nn.relu(hn @ w_ff1[layer])) @ w_ff2[layer]
    return rmsnorm(x, norm_final)
```

```python
import functools
import math

import numpy as np
import jax
import jax.numpy as jnp
from jax import lax
from jax.experimental import pallas as pl
from jax.experimental.pallas import tpu as pltpu

F32 = jnp.float32
BF16 = jnp.bfloat16

D_MODEL = 1024
SEQ = 2048
HEAD_DIM = 64
ATTN_HEADS = 8
DILATIONS = (1, 4, 16)
RADIUS = 64
N_GROUPS = len(DILATIONS)
ATTN_WIDTH = ATTN_HEADS * HEAD_DIM
ROPE_THETA = 10000.0
NEG_INF = -1e30
HYENA_WIDTH = D_MODEL // 2
FILTER_BANDS = 16
FILTER_EMB_DIM = 1 + 2 * FILTER_BANDS
FILTER_HIDDEN = 64
FILTER_INNER = 2
DECAY_TARGET = 1e-2
FAST_DECAY_PCT = 0.3
SLOW_DECAY_PCT = 1.5
HY_IN_WIDTH = 3 * HYENA_WIDTH
QKV_WIDTH = N_GROUPS * 3 * ATTN_WIDTH
IN_WIDTH = HY_IN_WIDTH + QKV_WIDTH + 2 * D_MODEL
D_FF = 4 * D_MODEL
RMS_EPS = 1e-6

LANES = 128
IN_TILE = 512
N_IN_TILES = IN_WIDTH // IN_TILE
HY_TILES = HY_IN_WIDTH // IN_TILE
GROUP_TILES = 3 * ATTN_WIDTH // IN_TILE
QKV_TILE0 = HY_TILES
GATE_TILE0 = QKV_TILE0 + N_GROUPS * GROUP_TILES
N_FFT = 2 * SEQ
FREQ_BLOCK = 512
N_FREQ_BLOCKS = SEQ // FREQ_BLOCK
Q_BLOCK = 128
K_WINDOW = 256
VMEM_LIMIT = 56 * 1024 * 1024


def _params(n_axes, vmem=VMEM_LIMIT):
    return pltpu.CompilerParams(dimension_semantics=("arbitrary",) * n_axes, vmem_limit_bytes=vmem)


def _perm_positions(dilation):
    n = SEQ // dilation
    r = np.arange(SEQ) // n
    m = np.arange(SEQ) % n
    return m * dilation + r


@functools.lru_cache(maxsize=None)
def _rope_tables():
    half = HEAD_DIM // 2
    inv_freq = ROPE_THETA ** (-np.arange(half, dtype=np.float64) / half)
    lane = np.arange(LANES)
    first_half = (lane % HEAD_DIM) < half
    cos_t, sa_t, sb_t = [], [], []
    for d in DILATIONS:
        ang = _perm_positions(d)[:, None].astype(np.float64) * inv_freq[None, :]
        cos_l = np.cos(ang)[:, lane % half]
        sin_l = np.sin(ang)[:, lane % half]
        cos_t.append(cos_l)
        sa_t.append(np.where(first_half[None, :], -sin_l, 0.0))
        sb_t.append(np.where(first_half[None, :], 0.0, sin_l))
    f = lambda a: np.stack(a).astype(np.float32)
    return f(cos_t), f(sa_t), f(sb_t)


@functools.lru_cache(maxsize=None)
def _dft_tables():
    k = np.arange(SEQ, dtype=np.int64)
    n = np.arange(SEQ, dtype=np.int64)
    idx = ((2 * k[:, None] + 1) * n[None, :]) % (2 * N_FFT)
    phi = idx.astype(np.float64) * (2.0 * math.pi / (2 * N_FFT))
    c = np.cos(phi).reshape(N_FREQ_BLOCKS, FREQ_BLOCK, SEQ)
    s = np.sin(phi).reshape(N_FREQ_BLOCKS, FREQ_BLOCK, SEQ)
    fwd = np.concatenate([c, -s], axis=1).reshape(2 * SEQ, SEQ)
    inv = (2.0 / N_FFT) * fwd.T
    return fwd.astype(np.float32), np.ascontiguousarray(inv).astype(np.float32)


@functools.lru_cache(maxsize=None)
def _filter_tables():
    n = np.arange(SEQ, dtype=np.float64)
    t = n / max(SEQ - 1, 1)
    bands = np.linspace(1e-4, FILTER_BANDS - 1, FILTER_BANDS)
    ang = (2.0 * math.pi / SEQ) * n[:, None] * bands[None, :]
    z = np.concatenate([t[:, None], np.cos(ang), -np.sin(ang)], axis=-1)
    z = np.pad(z, ((0, 0), (0, LANES - FILTER_EMB_DIM)))
    max_decay = math.log(DECAY_TARGET) / FAST_DECAY_PCT
    min_decay = math.log(DECAY_TARGET) / SLOW_DECAY_PCT
    deltas = np.abs(np.linspace(min_decay, max_decay, HYENA_WIDTH))
    window = np.exp(-t[:, None] * deltas[None, :])
    return z.astype(np.float32), window.astype(np.float32)


def _filter_kernel(z_ref, win_ref, w1_ref, b1_ref, wi_ref, bi_ref, wo_ref, fr_ref, f_ref,
                   kr_ref, ki_ref, hcat_ref):
    i = pl.program_id(0)
    rows_per_chunk = 256
    hi = lax.Precision.HIGHEST

    @pl.when(i == 0)
    def _():
        fr = fr_ref[...]

        def chunk(c, carry):
            r0 = pl.multiple_of(c * rows_per_chunk, rows_per_chunk)
            rows = pl.ds(r0, rows_per_chunk)
            hid = jnp.sin(fr * (jnp.dot(z_ref[rows, :], w1_ref[...], precision=hi,
                                        preferred_element_type=F32) + b1_ref[...]))
            for l in range(FILTER_INNER):
                hid = jnp.sin(fr * (jnp.dot(hid, wi_ref[l], precision=hi,
                                            preferred_element_type=F32) + bi_ref[l]))
            filt = jnp.dot(hid, wo_ref[...], precision=hi, preferred_element_type=F32)
            win = win_ref[rows, :]
            row = r0 + lax.broadcasted_iota(jnp.int32, (rows_per_chunk, 1), 0)
            h_fwd = filt[:, :HYENA_WIDTH] * win
            h_bwd = jnp.where(row == 0, 0.0, filt[:, HYENA_WIDTH:] * win)
            hcat_ref[rows, :HYENA_WIDTH] = h_fwd.astype(BF16)
            hcat_ref[rows, HYENA_WIDTH:] = h_bwd.astype(BF16)
            return carry

        lax.fori_loop(0, SEQ // rows_per_chunk, chunk, 0)

    sp = jnp.dot(f_ref[...], hcat_ref[...], preferred_element_type=F32)
    re, im = sp[:FREQ_BLOCK], sp[FREQ_BLOCK:]
    kr_ref[...] = re[:, :HYENA_WIDTH] + re[:, HYENA_WIDTH:]
    ki_ref[...] = im[:, :HYENA_WIDTH] - im[:, HYENA_WIDTH:]


def _hyena_filter_spectrum(z, window, w1, b1, w_inner, b_inner, w_out, freq, dft_fwd):
    full = lambda a: pl.BlockSpec(a.shape, lambda i: (0,) * a.ndim)
    w1p = jnp.pad(w1, ((0, LANES - FILTER_EMB_DIM), (0, 0)))
    b1 = b1.reshape(1, FILTER_HIDDEN)
    b_inner = b_inner.reshape(FILTER_INNER, 1, FILTER_HIDDEN)
    freq = freq.reshape(1, FILTER_HIDDEN)
    ops = (z, window, w1p, b1, w_inner, b_inner, w_out, freq)
    out_spec = pl.BlockSpec((FREQ_BLOCK, HYENA_WIDTH), lambda i: (i, 0))
    return pl.pallas_call(
        _filter_kernel,
        grid=(N_FREQ_BLOCKS,),
        in_specs=[full(a) for a in ops] + [pl.BlockSpec((2 * FREQ_BLOCK, SEQ), lambda i: (i, 0))],
        out_specs=[out_spec, out_spec],
        out_shape=[jax.ShapeDtypeStruct((SEQ, HYENA_WIDTH), F32)] * 2,
        scratch_shapes=[pltpu.VMEM((SEQ, 2 * HYENA_WIDTH), BF16)],
        compiler_params=_params(1),
        name="hyena_filter",
    )(*ops, dft_fwd)


def _tile_group(j):
    return jnp.clip((j - QKV_TILE0) // GROUP_TILES, 0, N_GROUPS - 1) * (j < GATE_TILE0).astype(jnp.int32)


def _in_proj_kernel(*refs):
    n_slabs = D_MODEL // LANES
    x_refs = refs[:n_slabs]
    g_ref, w_ref, cos_ref, sa_ref, sb_ref, o_ref, hn_ref = refs[n_slabs:]
    j = pl.program_id(1)

    @pl.when(j == 0)
    def _():
        gain = g_ref[...]

        def norm_rows(p, src_start, stride, dst_start, rows):
            idx = pl.ds(src_start, rows) if stride == 1 else pl.ds(src_start, rows, stride=stride)
            xs = jnp.concatenate([xr[idx, :] for xr in x_refs], axis=1)
            ms = jnp.mean(xs * xs, axis=1, keepdims=True)
            hn_ref[p, pl.ds(dst_start, rows), :] = (xs * lax.rsqrt(ms + RMS_EPS) * gain).astype(BF16)

        for p, d in enumerate(DILATIONS):
            n = SEQ // d
            rows = min(n, 256)
            per_seg = n // rows

            def chunk(c, carry, p=p, d=d, rows=rows, per_seg=per_seg):
                r = c // per_seg
                m0 = (c % per_seg) * rows
                norm_rows(p, r + d * m0, d, pl.multiple_of(c * rows, rows), rows)
                return carry

            lax.fori_loop(0, SEQ // rows, chunk, 0)

    grp = _tile_group(j)
    rel = j - QKV_TILE0
    is_rot = (rel >= 0) & (j < GATE_TILE0) & (lax.rem(rel, GROUP_TILES) != GROUP_TILES - 1)

    @pl.when(jnp.logical_not(is_rot))
    def _():
        o_ref[...] = jnp.dot(hn_ref[grp], w_ref[...], preferred_element_type=F32).astype(BF16)

    @pl.when(is_rot)
    def _():
        acc = jnp.dot(hn_ref[grp], w_ref[...], preferred_element_type=F32)
        scale = jnp.where(lax.rem(rel, GROUP_TILES) == 0, HEAD_DIM ** -0.5, 1.0)
        cos = cos_ref[0] * scale
        sa = sa_ref[0] * scale
        sb = sb_ref[0] * scale
        for c in range(IN_TILE // LANES):
            t = acc[:, c * LANES:(c + 1) * LANES]
            half = HEAD_DIM // 2
            out = t * cos + pltpu.roll(t, LANES - half, 1) * sa + pltpu.roll(t, half, 1) * sb
            o_ref[:, c * LANES:(c + 1) * LANES] = out.astype(BF16)


def _in_proj(x2d, gain, w_in, rope, batch):
    n_slabs = D_MODEL // LANES
    cos_t, sa_t, sb_t = rope
    x_specs = [pl.BlockSpec((SEQ, LANES), functools.partial(lambda b, j, c: (b, c), c=c)) for c in range(n_slabs)]
    rope_spec = pl.BlockSpec((1, SEQ, LANES), lambda b, j: (_tile_group(j), 0, 0))
    return pl.pallas_call(
        _in_proj_kernel,
        grid=(batch, N_IN_TILES),
        in_specs=x_specs + [
            pl.BlockSpec((1, D_MODEL), lambda b, j: (0, 0)),
            pl.BlockSpec((D_MODEL, IN_TILE), lambda b, j: (0, j)),
            rope_spec, rope_spec, rope_spec,
        ],
        out_specs=pl.BlockSpec((SEQ, IN_TILE), lambda b, j: (b, j)),
        out_shape=jax.ShapeDtypeStruct((batch * SEQ, IN_WIDTH), BF16),
        scratch_shapes=[pltpu.VMEM((N_GROUPS, SEQ, D_MODEL), BF16)],
        compiler_params=_params(2),
        name="in_proj",
    )(*([x2d] * n_slabs), gain.reshape(1, D_MODEL), w_in, cos_t, sa_t, sb_t)


def _short_conv(p_ref, w_ref, b_ref, lo):
    p = p_ref[...].astype(F32)
    row = lax.broadcasted_iota(jnp.int32, (SEQ, 1), 0)
    prev = jnp.where(row == 0, 0.0, pltpu.roll(p, 1, 0))
    nxt = jnp.where(row == SEQ - 1, 0.0, pltpu.roll(p, SEQ - 1, 0))
    w = w_ref[:, lo:lo + HYENA_WIDTH]
    return b_ref[:, lo:lo + HYENA_WIDTH] + prev * w[0:1] + p * w[1:2] + nxt * w[2:3]


def _hyena_kernel(x0_ref, x1_ref, v_ref, cw_ref, cb_ref, skip_ref, f_ref, g_ref, kr_ref, ki_ref,
                  o_ref, uv_ref, acc_ref):
    i = pl.program_id(1)

    @pl.when(i == 0)
    def _():
        uv = _short_conv(x1_ref, cw_ref, cb_ref, HYENA_WIDTH) * _short_conv(v_ref, cw_ref, cb_ref, 2 * HYENA_WIDTH)
        uv_ref[...] = uv.astype(BF16)
        acc_ref[...] = uv * skip_ref[...]

    sp = jnp.dot(f_ref[...], uv_ref[...], preferred_element_type=F32)
    re, im = sp[:FREQ_BLOCK], sp[FREQ_BLOCK:]
    kr, ki = kr_ref[...], ki_ref[...]
    prod = jnp.concatenate([re * kr - im * ki, re * ki + im * kr], axis=0).astype(BF16)
    acc_ref[...] += jnp.dot(g_ref[...], prod, preferred_element_type=F32)

    @pl.when(i == N_FREQ_BLOCKS - 1)
    def _():
        o_ref[...] = (_short_conv(x0_ref, cw_ref, cb_ref, 0) * acc_ref[...]).astype(BF16)


def _hyena(proj, conv_w, conv_b, skip, dft_fwd, dft_inv, kr, ki, batch):
    slab = lambda c: pl.BlockSpec((SEQ, HYENA_WIDTH), functools.partial(lambda b, i, c: (b, c), c=c))
    const = lambda a: pl.BlockSpec(a.shape, lambda b, i: (0,) * a.ndim)
    conv_b = conv_b.reshape(1, HY_IN_WIDTH)
    skip = skip.reshape(1, HYENA_WIDTH)
    kspec = pl.BlockSpec((FREQ_BLOCK, HYENA_WIDTH), lambda b, i: (i, 0))
    return pl.pallas_call(
        _hyena_kernel,
        grid=(batch, N_FREQ_BLOCKS),
        in_specs=[slab(0), slab(1), slab(2), const(conv_w), const(conv_b), const(skip),
                  pl.BlockSpec((2 * FREQ_BLOCK, SEQ), lambda b, i: (i, 0)),
                  pl.BlockSpec((SEQ, 2 * FREQ_BLOCK), lambda b, i: (0, i)),
                  kspec, kspec],
        out_specs=pl.BlockSpec((SEQ, HYENA_WIDTH), lambda b, i: (b, 0)),
        out_shape=jax.ShapeDtypeStruct((batch * SEQ, HYENA_WIDTH), BF16),
        scratch_shapes=[pltpu.VMEM((SEQ, HYENA_WIDTH), BF16), pltpu.VMEM((SEQ, HYENA_WIDTH), F32)],
        compiler_params=_params(2),
        name="hyena",
    )(proj, proj, proj, conv_w, conv_b, skip, dft_fwd, dft_inv, kr, ki)


def _attention_kernel(*refs):
    qkv_refs = refs[:3 * N_GROUPS]
    o_ref, acc_ref, max_ref = refs[3 * N_GROUPS:]
    lane = lax.broadcasted_iota(jnp.int32, (1, LANES), 1)
    head_lanes = (lane < HEAD_DIM, lane >= HEAD_DIM)
    rel = (lax.broadcasted_iota(jnp.int32, (Q_BLOCK, K_WINDOW), 1)
           - lax.broadcasted_iota(jnp.int32, (Q_BLOCK, K_WINDOW), 0))
    col = lax.broadcasted_iota(jnp.int32, (1, K_WINDOW), 1)
    far = 1 << 20

    for g, d in enumerate(DILATIONS):
        q_ref, k_ref, v_ref = qkv_refs[3 * g:3 * g + 3]
        n = SEQ // d

        def q_block(c, carry, g=g, d=d, n=n, q_ref=q_ref, k_ref=k_ref, v_ref=v_ref):
            q0 = pl.multiple_of(c * Q_BLOCK, Q_BLOCK)
            ws = pl.multiple_of(jnp.clip(q0 - RADIUS, 0, SEQ - K_WINDOW), RADIUS)
            seg = (q0 // n) * n
            kpos = col + ws
            in_seg = (kpos >= seg) & (kpos < seg + n)
            shift = jnp.where(in_seg, ws - q0, far)
            valid = jnp.abs(rel + shift) <= RADIUS
            q = q_ref[pl.ds(q0, Q_BLOCK), :]
            kw = k_ref[pl.ds(ws, K_WINDOW), :]
            vw = v_ref[pl.ds(ws, K_WINDOW), :]
            if d == 1:
                dst = pl.ds(q0, Q_BLOCK)
            else:
                dst = pl.ds(q0 // n + d * lax.rem(q0, n), Q_BLOCK, stride=d)
            for h in range(2):
                qh = jnp.where(head_lanes[h], q, jnp.zeros_like(q))
                s = lax.dot_general(qh, kw, (((1,), (1,)), ((), ())), preferred_element_type=F32)
                s = jnp.where(valid, s, NEG_INF)
                m = jnp.max(s, axis=1, keepdims=True)
                p = jnp.exp(s - m).astype(BF16)
                vh = jnp.where(head_lanes[h], vw, jnp.ones_like(vw))
                acc_ref[g, h, dst, :] = jnp.dot(p, vh, preferred_element_type=F32)
                max_ref[g, h, dst, :] = jnp.broadcast_to(m, (Q_BLOCK, LANES))
            return carry

        lax.fori_loop(0, SEQ // Q_BLOCK, q_block, 0)

    def combine(c, carry):
        rows = pl.ds(pl.multiple_of(c * Q_BLOCK, Q_BLOCK), Q_BLOCK)
        ys = []
        for h in range(2):
            ms = [max_ref[g, h, rows, :] for g in range(N_GROUPS)]
            top = functools.reduce(jnp.maximum, ms)
            tot = sum(jnp.exp(ms[g] - top) * acc_ref[g, h, rows, :] for g in range(N_GROUPS))
            ys.append(tot / pltpu.roll(tot, HEAD_DIM, 1))
        o_ref[rows, :] = jnp.where(head_lanes[0], ys[0], ys[1]).astype(BF16)
        return carry

    lax.fori_loop(0, SEQ // Q_BLOCK, combine, 0)


def _attention(proj, batch):
    pairs = ATTN_WIDTH // LANES
    qkv_block0 = HY_IN_WIDTH // LANES
    specs = []
    for g in range(N_GROUPS):
        for t in range(3):
            base = qkv_block0 + (3 * g + t) * pairs
            specs.append(pl.BlockSpec((SEQ, LANES), functools.partial(lambda b, hp, base: (b, base + hp), base=base)))
    return pl.pallas_call(
        _attention_kernel,
        grid=(batch, pairs),
        in_specs=specs,
        out_specs=pl.BlockSpec((SEQ, LANES), lambda b, hp: (b, hp)),
        out_shape=jax.ShapeDtypeStruct((batch * SEQ, ATTN_WIDTH), BF16),
        scratch_shapes=[pltpu.VMEM((N_GROUPS, 2, SEQ, LANES), F32), pltpu.VMEM((N_GROUPS, 2, SEQ, LANES), F32)],
        compiler_params=_params(2),
        name="attention",
    )(*([proj] * (3 * N_GROUPS)))


def _merge_kernel(x_ref, yh_ref, ya_ref, gh_ref, ga_ref, ph_ref, pa_ref, wo_ref, o_ref):
    a = jnp.dot(yh_ref[...], ph_ref[...], preferred_element_type=F32)
    b = jnp.dot(ya_ref[...], pa_ref[...], preferred_element_type=F32)
    merged = jax.nn.sigmoid(gh_ref[...].astype(F32)) * a + jax.nn.sigmoid(ga_ref[...].astype(F32)) * b
    o_ref[...] = x_ref[...] + jnp.dot(merged.astype(BF16), wo_ref[...], preferred_element_type=F32)


def _merge(x2d, y_hy, y_att, proj, p_hy, p_att, w_o, rows):
    tm = 1024
    gate_block0 = (HY_IN_WIDTH + QKV_WIDTH) // D_MODEL
    row = lambda w: pl.BlockSpec((tm, w), lambda i: (i, 0))
    const = lambda a: pl.BlockSpec(a.shape, lambda i: (0, 0))
    return pl.pallas_call(
        _merge_kernel,
        grid=(rows // tm,),
        in_specs=[row(D_MODEL), row(HYENA_WIDTH), row(ATTN_WIDTH),
                  pl.BlockSpec((tm, D_MODEL), lambda i: (i, gate_block0)),
                  pl.BlockSpec((tm, D_MODEL), lambda i: (i, gate_block0 + 1)),
                  const(p_hy), const(p_att), const(w_o)],
        out_specs=row(D_MODEL),
        out_shape=jax.ShapeDtypeStruct((rows, D_MODEL), F32),
        compiler_params=_params(1),
        name="merge",
    )(x2d, y_hy, y_att, proj, proj, p_hy, p_att, w_o)


def _rms(x, gain):
    return x * lax.rsqrt(jnp.mean(x * x, axis=-1, keepdims=True) + RMS_EPS) * gain


def _ffn_kernel(x_ref, g_ref, w1_ref, w2_ref, gf_ref, o_ref, *, final_norm):
    x = x_ref[...]
    hn = _rms(x, g_ref[...]).astype(BF16)
    chunk = 1024
    y = x
    for c in range(D_FF // chunk):
        h = jnp.dot(hn, w1_ref[:, c * chunk:(c + 1) * chunk], preferred_element_type=F32)
        h = jnp.square(jnp.maximum(h, 0.0)).astype(BF16)
        y = y + jnp.dot(h, w2_ref[c * chunk:(c + 1) * chunk, :], preferred_element_type=F32)
    o_ref[...] = _rms(y, gf_ref[...]) if final_norm else y


def _ffn(x2d, gain, w1, w2, gain_final, rows, final_norm):
    tm = 512
    row = pl.BlockSpec((tm, D_MODEL), lambda i: (i, 0))
    const = lambda a: pl.BlockSpec(a.shape, lambda i: (0, 0))
    gain = gain.reshape(1, D_MODEL)
    gain_final = gain_final.reshape(1, D_MODEL)
    return pl.pallas_call(
        functools.partial(_ffn_kernel, final_norm=final_norm),
        grid=(rows // tm,),
        in_specs=[row, const(gain), const(w1), const(w2), const(gain_final)],
        out_specs=row,
        out_shape=jax.ShapeDtypeStruct((rows, D_MODEL), F32),
        compiler_params=_params(1),
        name="ffn",
    )(x2d, gain, w1, w2, gain_final)


def kernel(x, norm_mix, w_in, conv_w, conv_b, filt_w1, filt_b1, filt_w_inner, filt_b_inner, filt_w_out,
           filt_freq, hy_skip, p_hy, p_att, w_o, norm_ffn, w_ff1, w_ff2, norm_final):
    batch, seq, d_model = x.shape
    assert (seq, d_model) == (SEQ, D_MODEL)
    depth = norm_mix.shape[0]
    rows = batch * seq
    rope = tuple(jnp.asarray(t) for t in _rope_tables())
    dft_fwd, dft_inv = (jnp.asarray(t).astype(BF16) for t in _dft_tables())
    z, window = (jnp.asarray(t) for t in _filter_tables())
    bf = lambda a: a.astype(BF16)

    x2d = x.reshape(rows, d_model)
    for l in range(depth):
        kr, ki = _hyena_filter_spectrum(z, window, filt_w1[l], filt_b1[l], filt_w_inner[l], filt_b_inner[l],
                                        filt_w_out[l], filt_freq[l], dft_fwd)
        proj = _in_proj(x2d, norm_mix[l], bf(w_in[l]), rope, batch)
        y_hy = _hyena(proj, conv_w[l], conv_b[l], hy_skip[l], dft_fwd, dft_inv, kr, ki, batch)
        y_att = _attention(proj, batch)
        x2d = _merge(x2d, y_hy, y_att, proj, bf(p_hy[l]), bf(p_att[l]), bf(w_o[l]), rows)
        x2d = _ffn(x2d, norm_ffn[l], bf(w_ff1[l]), bf(w_ff2[l]), norm_final, rows, final_norm=(l == depth - 1))
    return x2d.reshape(batch, seq, d_model)
```

```python
import functools
import math

import numpy as np
import jax
import jax.numpy as jnp
from jax import lax
from jax.experimental import pallas as pl
from jax.experimental.pallas import tpu as pltpu

F32 = jnp.float32
BF16 = jnp.bfloat16

D_MODEL = 1024
SEQ = 2048
HEAD_DIM = 64
ATTN_HEADS = 8
DILATIONS = (1, 4, 16)
RADIUS = 64
N_GROUPS = len(DILATIONS)
ATTN_WIDTH = ATTN_HEADS * HEAD_DIM
ROPE_THETA = 10000.0
NEG_INF = -1e30
HYENA_WIDTH = D_MODEL // 2
FILTER_BANDS = 16
FILTER_EMB_DIM = 1 + 2 * FILTER_BANDS
FILTER_HIDDEN = 64
FILTER_INNER = 2
DECAY_TARGET = 1e-2
FAST_DECAY_PCT = 0.3
SLOW_DECAY_PCT = 1.5
HY_IN_WIDTH = 3 * HYENA_WIDTH
QKV_WIDTH = N_GROUPS * 3 * ATTN_WIDTH
IN_WIDTH = HY_IN_WIDTH + QKV_WIDTH + 2 * D_MODEL
D_FF = 4 * D_MODEL
RMS_EPS = 1e-6

LANES = 128
IN_TILE = 512
N_IN_TILES = IN_WIDTH // IN_TILE
HY_TILES = HY_IN_WIDTH // IN_TILE
GROUP_TILES = 3 * ATTN_WIDTH // IN_TILE
QKV_TILE0 = HY_TILES
GATE_TILE0 = QKV_TILE0 + N_GROUPS * GROUP_TILES
N_FFT = 2 * SEQ
FREQ_BLOCK = 512
N_FREQ_BLOCKS = SEQ // FREQ_BLOCK
Q_BLOCK = 128
K_WINDOW = 256
VMEM_LIMIT = 56 * 1024 * 1024


def _params(n_axes, vmem=VMEM_LIMIT):
    return pltpu.CompilerParams(dimension_semantics=("arbitrary",) * n_axes, vmem_limit_bytes=vmem)


def _perm_positions(dilation):
    n = SEQ // dilation
    r = np.arange(SEQ) // n
    m = np.arange(SEQ) % n
    return m * dilation + r


@functools.lru_cache(maxsize=None)
def _rope_tables():
    half = HEAD_DIM // 2
    inv_freq = ROPE_THETA ** (-np.arange(half, dtype=np.float64) / half)
    lane = np.arange(LANES)
    first_half = (lane % HEAD_DIM) < half
    cos_t, sa_t, sb_t = [], [], []
    for d in DILATIONS:
        ang = _perm_positions(d)[:, None].astype(np.float64) * inv_freq[None, :]
        cos_l = np.cos(ang)[:, lane % half]
        sin_l = np.sin(ang)[:, lane % half]
        cos_t.append(cos_l)
        sa_t.append(np.where(first_half[None, :], -sin_l, 0.0))
        sb_t.append(np.where(first_half[None, :], 0.0, sin_l))
    f = lambda a: np.stack(a).astype(np.float32)
    return f(cos_t), f(sa_t), f(sb_t)


@functools.lru_cache(maxsize=None)
def _dft_tables():
    k = np.arange(SEQ, dtype=np.int64)
    n = np.arange(SEQ, dtype=np.int64)
    idx = ((2 * k[:, None] + 1) * n[None, :]) % (2 * N_FFT)
    phi = idx.astype(np.float64) * (2.0 * math.pi / (2 * N_FFT))
    c = np.cos(phi).reshape(N_FREQ_BLOCKS, FREQ_BLOCK, SEQ)
    s = np.sin(phi).reshape(N_FREQ_BLOCKS, FREQ_BLOCK, SEQ)
    fwd = np.concatenate([c, -s], axis=1).reshape(2 * SEQ, SEQ)
    inv = (2.0 / N_FFT) * fwd.T
    return fwd.astype(np.float32), np.ascontiguousarray(inv).astype(np.float32)


@functools.lru_cache(maxsize=None)
def _filter_tables():
    n = np.arange(SEQ, dtype=np.float64)
    t = n / max(SEQ - 1, 1)
    bands = np.linspace(1e-4, FILTER_BANDS - 1, FILTER_BANDS)
    ang = (2.0 * math.pi / SEQ) * n[:, None] * bands[None, :]
    z = np.concatenate([t[:, None], np.cos(ang), -np.sin(ang)], axis=-1)
    z = np.pad(z, ((0, 0), (0, LANES - FILTER_EMB_DIM)))
    max_decay = math.log(DECAY_TARGET) / FAST_DECAY_PCT
    min_decay = math.log(DECAY_TARGET) / SLOW_DECAY_PCT
    deltas = np.abs(np.linspace(min_decay, max_decay, HYENA_WIDTH))
    window = np.exp(-t[:, None] * deltas[None, :])
    return z.astype(np.float32), window.astype(np.float32)


def _filter_kernel(z_ref, win_ref, w1_ref, b1_ref, wi_ref, bi_ref, wo_ref, fr_ref, f_ref,
                   kr_ref, ki_ref, hcat_ref):
    i = pl.program_id(0)
    rows_per_chunk = 256
    hi = lax.Precision.HIGHEST

    @pl.when(i == 0)
    def _():
        fr = fr_ref[...]

        def chunk(c, carry):
            r0 = pl.multiple_of(c * rows_per_chunk, rows_per_chunk)
            rows = pl.ds(r0, rows_per_chunk)
            hid = jnp.sin(fr * (jnp.dot(z_ref[rows, :], w1_ref[...], precision=hi,
                                        preferred_element_type=F32) + b1_ref[...]))
            for l in range(FILTER_INNER):
                hid = jnp.sin(fr * (jnp.dot(hid, wi_ref[l], precision=hi,
                                            preferred_element_type=F32) + bi_ref[l]))
            filt = jnp.dot(hid, wo_ref[...], precision=hi, preferred_element_type=F32)
            win = win_ref[rows, :]
            row = r0 + lax.broadcasted_iota(jnp.int32, (rows_per_chunk, 1), 0)
            h_fwd = filt[:, :HYENA_WIDTH] * win
            h_bwd = jnp.where(row == 0, 0.0, filt[:, HYENA_WIDTH:] * win)
            hcat_ref[rows, :HYENA_WIDTH] = h_fwd.astype(BF16)
            hcat_ref[rows, HYENA_WIDTH:] = h_bwd.astype(BF16)
            return carry

        lax.fori_loop(0, SEQ // rows_per_chunk, chunk, 0)

    sp = jnp.dot(f_ref[...], hcat_ref[...], preferred_element_type=F32)
    re, im = sp[:FREQ_BLOCK], sp[FREQ_BLOCK:]
    kr_ref[...] = re[:, :HYENA_WIDTH] + re[:, HYENA_WIDTH:]
    ki_ref[...] = im[:, :HYENA_WIDTH] - im[:, HYENA_WIDTH:]


def _hyena_filter_spectrum(z, window, w1, b1, w_inner, b_inner, w_out, freq, dft_fwd):
    full = lambda a: pl.BlockSpec(a.shape, lambda i: (0,) * a.ndim)
    w1p = jnp.pad(w1, ((0, LANES - FILTER_EMB_DIM), (0, 0)))
    b1 = b1.reshape(1, FILTER_HIDDEN)
    b_inner = b_inner.reshape(FILTER_INNER, 1, FILTER_HIDDEN)
    freq = freq.reshape(1, FILTER_HIDDEN)
    ops = (z, window, w1p, b1, w_inner, b_inner, w_out, freq)
    out_spec = pl.BlockSpec((FREQ_BLOCK, HYENA_WIDTH), lambda i: (i, 0))
    return pl.pallas_call(
        _filter_kernel,
        grid=(N_FREQ_BLOCKS,),
        in_specs=[full(a) for a in ops] + [pl.BlockSpec((2 * FREQ_BLOCK, SEQ), lambda i: (i, 0))],
        out_specs=[out_spec, out_spec],
        out_shape=[jax.ShapeDtypeStruct((SEQ, HYENA_WIDTH), F32)] * 2,
        scratch_shapes=[pltpu.VMEM((SEQ, 2 * HYENA_WIDTH), BF16)],
        compiler_params=_params(1),
        name="hyena_filter",
    )(*ops, dft_fwd)


def _tile_group(j):
    return jnp.clip((j - QKV_TILE0) // GROUP_TILES, 0, N_GROUPS - 1) * (j < GATE_TILE0).astype(jnp.int32)


def _in_proj_kernel(*refs):
    n_slabs = D_MODEL // LANES
    x_refs = refs[:n_slabs]
    g_ref, w_ref, cos_ref, sa_ref, sb_ref, o_ref, hn_ref, n4_ref = refs[n_slabs:]
    j = pl.program_id(1)
    d1, d2 = DILATIONS[1], DILATIONS[2]
    assert DILATIONS[0] == 1 and d2 == d1 * d1

    @pl.when(j == 0)
    def _():
        gain = g_ref[...]
        rows = 256

        def normalised(idx):
            xs = jnp.concatenate([xr[idx, :] for xr in x_refs], axis=1)
            ms = jnp.mean(xs * xs, axis=1, keepdims=True)
            return xs * lax.rsqrt(ms + RMS_EPS) * gain

        def plain(c, carry):
            dst = pl.ds(pl.multiple_of(c * rows, rows), rows)
            hn_ref[0, dst, :] = normalised(dst).astype(BF16)
            return carry

        lax.fori_loop(0, SEQ // rows, plain, 0)

        per_seg = SEQ // d1 // rows

        def level1(c, carry):
            src = pl.ds(c // per_seg + d1 * (c % per_seg) * rows, rows, stride=d1)
            dst = pl.ds(pl.multiple_of(c * rows, rows), rows)
            hn = normalised(src)
            hn_ref[1, dst, :] = hn.astype(BF16)
            for s in range(n_slabs):
                n4_ref[s, dst, :] = hn[:, s * LANES:(s + 1) * LANES]
            return carry

        lax.fori_loop(0, SEQ // rows, level1, 0)

        seg2 = SEQ // d2

        def level2(r, carry):
            src = pl.ds((r % d1) * (SEQ // d1) + r // d1, seg2, stride=d1)
            dst = pl.ds(pl.multiple_of(r * seg2, seg2), seg2)
            hn_ref[2, dst, :] = jnp.concatenate([n4_ref[s, src, :] for s in range(n_slabs)], axis=1).astype(BF16)
            return carry

        lax.fori_loop(0, d2, level2, 0)

    grp = _tile_group(j)
    rel = j - QKV_TILE0
    is_rot = (rel >= 0) & (j < GATE_TILE0) & (lax.rem(rel, GROUP_TILES) != GROUP_TILES - 1)
    row_chunk = 512

    @pl.when(jnp.logical_not(is_rot))
    def _():
        for rc in range(SEQ // row_chunk):
            rs = slice(rc * row_chunk, (rc + 1) * row_chunk)
            o_ref[rs, :] = jnp.dot(hn_ref[grp, rs, :], w_ref[...], preferred_element_type=F32).astype(BF16)

    @pl.when(is_rot)
    def _():
        scale = jnp.where(lax.rem(rel, GROUP_TILES) == 0, HEAD_DIM ** -0.5 * math.log2(math.e), 1.0)
        half = HEAD_DIM // 2
        for rc in range(SEQ // row_chunk):
            rs = slice(rc * row_chunk, (rc + 1) * row_chunk)
            acc = jnp.dot(hn_ref[grp, rs, :], w_ref[...], preferred_element_type=F32) * scale
            cos, sa, sb = cos_ref[0, rs, :], sa_ref[0, rs, :], sb_ref[0, rs, :]
            for c in range(IN_TILE // LANES):
                t = acc[:, c * LANES:(c + 1) * LANES]
                out = t * cos + pltpu.roll(t, LANES - half, 1) * sa + pltpu.roll(t, half, 1) * sb
                o_ref[rs, c * LANES:(c + 1) * LANES] = out.astype(BF16)


def _in_proj(x2d, gain, w_in, rope, batch):
    n_slabs = D_MODEL // LANES
    cos_t, sa_t, sb_t = rope
    x_specs = [pl.BlockSpec((SEQ, LANES), functools.partial(lambda b, j, c: (b, c), c=c)) for c in range(n_slabs)]
    rope_spec = pl.BlockSpec((1, SEQ, LANES), lambda b, j: (_tile_group(j), 0, 0))
    return pl.pallas_call(
        _in_proj_kernel,
        grid=(batch, N_IN_TILES),
        in_specs=x_specs + [
            pl.BlockSpec((1, D_MODEL), lambda b, j: (0, 0)),
            pl.BlockSpec((D_MODEL, IN_TILE), lambda b, j: (0, j)),
            rope_spec, rope_spec, rope_spec,
        ],
        out_specs=pl.BlockSpec((SEQ, IN_TILE), lambda b, j: (b, j)),
        out_shape=jax.ShapeDtypeStruct((batch * SEQ, IN_WIDTH), BF16),
        scratch_shapes=[pltpu.VMEM((N_GROUPS, SEQ, D_MODEL), BF16), pltpu.VMEM((n_slabs, SEQ, LANES), F32)],
        compiler_params=_params(2),
        name="in_proj",
    )(*([x2d] * n_slabs), gain.reshape(1, D_MODEL), w_in, cos_t, sa_t, sb_t)


def _short_conv(p_ref, w_ref, b_ref, lo):
    p = p_ref[...].astype(F32)
    row = lax.broadcasted_iota(jnp.int32, (SEQ, 1), 0)
    prev = jnp.where(row == 0, 0.0, pltpu.roll(p, 1, 0))
    nxt = jnp.where(row == SEQ - 1, 0.0, pltpu.roll(p, SEQ - 1, 0))
    w = w_ref[:, lo:lo + HYENA_WIDTH]
    return b_ref[:, lo:lo + HYENA_WIDTH] + prev * w[0:1] + p * w[1:2] + nxt * w[2:3]


def _hyena_kernel(x0_ref, x1_ref, v_ref, cw_ref, cb_ref, skip_ref, f_ref, g_ref, kr_ref, ki_ref,
                  o_ref, uv_ref, acc_ref):
    i = pl.program_id(1)

    @pl.when(i == 0)
    def _():
        uv = _short_conv(x1_ref, cw_ref, cb_ref, HYENA_WIDTH) * _short_conv(v_ref, cw_ref, cb_ref, 2 * HYENA_WIDTH)
        uv_ref[...] = uv.astype(BF16)
        acc_ref[...] = uv * skip_ref[...]

    sp = jnp.dot(f_ref[...], uv_ref[...], preferred_element_type=F32)
    re, im = sp[:FREQ_BLOCK], sp[FREQ_BLOCK:]
    kr, ki = kr_ref[...], ki_ref[...]
    prod = jnp.concatenate([re * kr - im * ki, re * ki + im * kr], axis=0).astype(BF16)
    acc_ref[...] += jnp.dot(g_ref[...], prod, preferred_element_type=F32)

    @pl.when(i == N_FREQ_BLOCKS - 1)
    def _():
        o_ref[...] = (_short_conv(x0_ref, cw_ref, cb_ref, 0) * acc_ref[...]).astype(BF16)


def _hyena(proj, conv_w, conv_b, skip, dft_fwd, dft_inv, kr, ki, batch):
    slab = lambda c: pl.BlockSpec((SEQ, HYENA_WIDTH), functools.partial(lambda b, i, c: (b, c), c=c))
    const = lambda a: pl.BlockSpec(a.shape, lambda b, i: (0,) * a.ndim)
    conv_b = conv_b.reshape(1, HY_IN_WIDTH)
    skip = skip.reshape(1, HYENA_WIDTH)
    kspec = pl.BlockSpec((FREQ_BLOCK, HYENA_WIDTH), lambda b, i: (i, 0))
    return pl.pallas_call(
        _hyena_kernel,
        grid=(batch, N_FREQ_BLOCKS),
        in_specs=[slab(0), slab(1), slab(2), const(conv_w), const(conv_b), const(skip),
                  pl.BlockSpec((2 * FREQ_BLOCK, SEQ), lambda b, i: (i, 0)),
                  pl.BlockSpec((SEQ, 2 * FREQ_BLOCK), lambda b, i: (0, i)),
                  kspec, kspec],
        out_specs=pl.BlockSpec((SEQ, HYENA_WIDTH), lambda b, i: (b, 0)),
        out_shape=jax.ShapeDtypeStruct((batch * SEQ, HYENA_WIDTH), BF16),
        scratch_shapes=[pltpu.VMEM((SEQ, HYENA_WIDTH), BF16), pltpu.VMEM((SEQ, HYENA_WIDTH), F32)],
        compiler_params=_params(2),
        name="hyena",
    )(proj, proj, proj, conv_w, conv_b, skip, dft_fwd, dft_inv, kr, ki)


@functools.lru_cache(maxsize=None)
def _attention_masks():
    i = np.arange(Q_BLOCK)[:, None]
    j = np.arange(K_WINDOW)[None, :]
    classes, ids = {}, []
    for d in DILATIONS:
        n = SEQ // d
        for c in range(SEQ // Q_BLOCK):
            q0 = c * Q_BLOCK
            ws = min(max(q0 - RADIUS, 0), SEQ - K_WINDOW)
            seg = (q0 // n) * n
            kpos = ws + j
            valid = (np.abs(kpos - (q0 + i)) <= RADIUS) & (kpos >= seg) & (kpos < seg + n)
            bias = np.where(valid, 0.0, NEG_INF).astype(np.float32)
            ids.append(classes.setdefault(bias.tobytes(), len(classes)))
    table = np.stack([np.frombuffer(b, np.float32).reshape(Q_BLOCK, K_WINDOW) for b in classes])
    return table, np.asarray(ids, np.int32)


def _attention_kernel(cls_ref, bias_ref, *refs):
    qkv_refs = refs[:3 * N_GROUPS]
    o_ref, acc_ref, den_ref, max_ref = refs[3 * N_GROUPS:]
    lane = lax.broadcasted_iota(jnp.int32, (1, LANES), 1)
    first = lane < HEAD_DIM
    ones = jnp.ones((K_WINDOW, LANES), BF16)
    n_blocks = SEQ // Q_BLOCK

    for g, d in enumerate(DILATIONS):
        q_ref, k_ref, v_ref = qkv_refs[3 * g:3 * g + 3]
        n = SEQ // d

        def q_block(c, carry, g=g, d=d, n=n, q_ref=q_ref, k_ref=k_ref, v_ref=v_ref):
            q0 = pl.multiple_of(c * Q_BLOCK, Q_BLOCK)
            ws = pl.multiple_of(jnp.clip(q0 - RADIUS, 0, SEQ - K_WINDOW), RADIUS)
            q = q_ref[pl.ds(q0, Q_BLOCK), :]
            zero = jnp.zeros_like(q)
            q2 = jnp.concatenate([jnp.where(first, q, zero), jnp.where(first, zero, q)], axis=0)
            s = lax.dot_general(q2, k_ref[pl.ds(ws, K_WINDOW), :], (((1,), (1,)), ((), ())),
                                preferred_element_type=F32)
            bias = bias_ref[cls_ref[g * n_blocks + c]]
            s = s + jnp.concatenate([bias, bias], axis=0)
            m = jnp.max(s, axis=1, keepdims=True)
            p = jnp.exp2(s - m).astype(BF16)
            out = jnp.dot(p, jnp.concatenate([v_ref[pl.ds(ws, K_WINDOW), :], ones], axis=1),
                          preferred_element_type=F32)
            if d == 1:
                dst = pl.ds(q0, Q_BLOCK)
            else:
                dst = pl.ds(q0 // n + d * lax.rem(q0, n), Q_BLOCK, stride=d)
            acc_ref[g, dst, :] = jnp.where(first, out[:Q_BLOCK, :LANES], out[Q_BLOCK:, :LANES])
            den_ref[g, dst, :] = jnp.where(first, out[:Q_BLOCK, LANES:], out[Q_BLOCK:, LANES:])
            max_ref[g, dst, :] = jnp.where(first, m[:Q_BLOCK], m[Q_BLOCK:])
            return carry

        lax.fori_loop(0, n_blocks, q_block, 0, unroll=16)

    def combine(c, carry):
        rows = pl.ds(pl.multiple_of(c * Q_BLOCK, Q_BLOCK), Q_BLOCK)
        ms = [max_ref[g, rows, :] for g in range(N_GROUPS)]
        top = functools.reduce(jnp.maximum, ms)
        ws = [jnp.exp2(m - top) for m in ms]
        num = sum(ws[g] * acc_ref[g, rows, :] for g in range(N_GROUPS))
        den = sum(ws[g] * den_ref[g, rows, :] for g in range(N_GROUPS))
        o_ref[rows, :] = (num / den).astype(BF16)
        return carry

    lax.fori_loop(0, n_blocks, combine, 0, unroll=2)


def _attention(proj, batch):
    pairs = ATTN_WIDTH // LANES
    qkv_block0 = HY_IN_WIDTH // LANES
    table, ids = (jnp.asarray(a) for a in _attention_masks())
    specs = [pl.BlockSpec(memory_space=pltpu.SMEM),
             pl.BlockSpec(table.shape, lambda b, hp: (0, 0, 0))]
    for g in range(N_GROUPS):
        for t in range(3):
            base = qkv_block0 + (3 * g + t) * pairs
            specs.append(pl.BlockSpec((SEQ, LANES), functools.partial(lambda b, hp, base: (b, base + hp), base=base)))
    group_rows = pltpu.VMEM((N_GROUPS, SEQ, LANES), F32)
    return pl.pallas_call(
        _attention_kernel,
        grid=(batch, pairs),
        in_specs=specs,
        out_specs=pl.BlockSpec((SEQ, LANES), lambda b, hp: (b, hp)),
        out_shape=jax.ShapeDtypeStruct((batch * SEQ, ATTN_WIDTH), BF16),
        scratch_shapes=[group_rows, group_rows, group_rows],
        compiler_params=_params(2),
        name="attention",
    )(ids, table, *([proj] * (3 * N_GROUPS)))


def _merge_kernel(x_ref, yh_ref, ya_ref, gh_ref, ga_ref, ph_ref, pa_ref, wo_ref, o_ref):
    a = jnp.dot(yh_ref[...], ph_ref[...], preferred_element_type=F32)
    b = jnp.dot(ya_ref[...], pa_ref[...], preferred_element_type=F32)
    merged = jax.nn.sigmoid(gh_ref[...].astype(F32)) * a + jax.nn.sigmoid(ga_ref[...].astype(F32)) * b
    o_ref[...] = x_ref[...] + jnp.dot(merged.astype(BF16), wo_ref[...], preferred_element_type=F32)


def _merge(x2d, y_hy, y_att, proj, p_hy, p_att, w_o, rows):
    tm = 1024
    gate_block0 = (HY_IN_WIDTH + QKV_WIDTH) // D_MODEL
    row = lambda w: pl.BlockSpec((tm, w), lambda i: (i, 0))
    const = lambda a: pl.BlockSpec(a.shape, lambda i: (0, 0))
    return pl.pallas_call(
        _merge_kernel,
        grid=(rows // tm,),
        in_specs=[row(D_MODEL), row(HYENA_WIDTH), row(ATTN_WIDTH),
                  pl.BlockSpec((tm, D_MODEL), lambda i: (i, gate_block0)),
                  pl.BlockSpec((tm, D_MODEL), lambda i: (i, gate_block0 + 1)),
                  const(p_hy), const(p_att), const(w_o)],
        out_specs=row(D_MODEL),
        out_shape=jax.ShapeDtypeStruct((rows, D_MODEL), F32),
        compiler_params=_params(1),
        name="merge",
    )(x2d, y_hy, y_att, proj, proj, p_hy, p_att, w_o)


def _rms(x, gain):
    return x * lax.rsqrt(jnp.mean(x * x, axis=-1, keepdims=True) + RMS_EPS) * gain


def _ffn_kernel(x_ref, g_ref, w1_ref, w2_ref, gf_ref, o_ref, *, final_norm):
    x = x_ref[...]
    hn = _rms(x, g_ref[...]).astype(BF16)
    chunk = 1024
    y = x
    for c in range(D_FF // chunk):
        h = jnp.dot(hn, w1_ref[:, c * chunk:(c + 1) * chunk], preferred_element_type=F32)
        h = jnp.square(jnp.maximum(h, 0.0)).astype(BF16)
        y = y + jnp.dot(h, w2_ref[c * chunk:(c + 1) * chunk, :], preferred_element_type=F32)
    o_ref[...] = _rms(y, gf_ref[...]) if final_norm else y


def _ffn(x2d, gain, w1, w2, gain_final, rows, final_norm):
    tm = 512
    row = pl.BlockSpec((tm, D_MODEL), lambda i: (i, 0))
    const = lambda a: pl.BlockSpec(a.shape, lambda i: (0, 0))
    gain = gain.reshape(1, D_MODEL)
    gain_final = gain_final.reshape(1, D_MODEL)
    return pl.pallas_call(
        functools.partial(_ffn_kernel, final_norm=final_norm),
        grid=(rows // tm,),
        in_specs=[row, const(gain), const(w1), const(w2), const(gain_final)],
        out_specs=row,
        out_shape=jax.ShapeDtypeStruct((rows, D_MODEL), F32),
        compiler_params=_params(1),
        name="ffn",
    )(x2d, gain, w1, w2, gain_final)


def kernel(x, norm_mix, w_in, conv_w, conv_b, filt_w1, filt_b1, filt_w_inner, filt_b_inner, filt_w_out,
           filt_freq, hy_skip, p_hy, p_att, w_o, norm_ffn, w_ff1, w_ff2, norm_final):
    batch, seq, d_model = x.shape
    assert (seq, d_model) == (SEQ, D_MODEL)
    depth = norm_mix.shape[0]
    rows = batch * seq
    rope = tuple(jnp.asarray(t) for t in _rope_tables())
    dft_fwd, dft_inv = (jnp.asarray(t).astype(BF16) for t in _dft_tables())
    z, window = (jnp.asarray(t) for t in _filter_tables())
    bf = lambda a: a.astype(BF16)

    x2d = x.reshape(rows, d_model)
    for l in range(depth):
        kr, ki = _hyena_filter_spectrum(z, window, filt_w1[l], filt_b1[l], filt_w_inner[l], filt_b_inner[l],
                                        filt_w_out[l], filt_freq[l], dft_fwd)
        proj = _in_proj(x2d, norm_mix[l], bf(w_in[l]), rope, batch)
        y_hy = _hyena(proj, conv_w[l], conv_b[l], hy_skip[l], dft_fwd, dft_inv, kr, ki, batch)
        y_att = _attention(proj, batch)
        x2d = _merge(x2d, y_hy, y_att, proj, bf(p_hy[l]), bf(p_att[l]), bf(w_o[l]), rows)
        x2d = _ffn(x2d, norm_ffn[l], bf(w_ff1[l]), bf(w_ff2[l]), norm_final, rows, final_norm=(l == depth - 1))
    return x2d.reshape(batch, seq, d_model)
```

```python
import functools
import math

import numpy as np
import jax
import jax.numpy as jnp
from jax import lax
from jax.experimental import pallas as pl
from jax.experimental.pallas import tpu as pltpu

F32 = jnp.float32
BF16 = jnp.bfloat16

D_MODEL = 1024
SEQ = 2048
HEAD_DIM = 64
ATTN_HEADS = 8
DILATIONS = (1, 4, 16)
RADIUS = 64
N_GROUPS = len(DILATIONS)
ATTN_WIDTH = ATTN_HEADS * HEAD_DIM
ROPE_THETA = 10000.0
NEG_INF = -1e30
HYENA_WIDTH = D_MODEL // 2
FILTER_BANDS = 16
FILTER_EMB_DIM = 1 + 2 * FILTER_BANDS
FILTER_HIDDEN = 64
FILTER_INNER = 2
DECAY_TARGET = 1e-2
FAST_DECAY_PCT = 0.3
SLOW_DECAY_PCT = 1.5
HY_IN_WIDTH = 3 * HYENA_WIDTH
GROUP_WIDTH = 3 * ATTN_WIDTH
QKV_WIDTH = N_GROUPS * GROUP_WIDTH
D_FF = 4 * D_MODEL
RMS_EPS = 1e-6

LANES = 128
N_SLABS = D_MODEL // LANES
N_FFT = 2 * SEQ
FREQ_BLOCK = 512
N_FREQ_BLOCKS = SEQ // FREQ_BLOCK
Q_BLOCK = 128
K_WINDOW = 256
ROW_TILE = 512
ROW_CHUNK = 512
VMEM_LIMIT = 56 * 1024 * 1024

assert DILATIONS[0] == 1 and DILATIONS[2] == DILATIONS[1] ** 2


def _params(n_axes, vmem=VMEM_LIMIT):
    return pltpu.CompilerParams(dimension_semantics=("arbitrary",) * n_axes, vmem_limit_bytes=vmem)


def _perm_positions(dilation):
    n = SEQ // dilation
    r = np.arange(SEQ) // n
    m = np.arange(SEQ) % n
    return m * dilation + r


@functools.lru_cache(maxsize=None)
def _rope_tables():
    half = HEAD_DIM // 2
    inv_freq = ROPE_THETA ** (-np.arange(half, dtype=np.float64) / half)
    lane = np.arange(LANES)
    first_half = (lane % HEAD_DIM) < half
    tables = []
    for d in DILATIONS:
        ang = _perm_positions(d)[:, None].astype(np.float64) * inv_freq[None, :]
        cos_l = np.cos(ang)[:, lane % half]
        sin_l = np.sin(ang)[:, lane % half]
        sa = np.where(first_half[None, :], -sin_l, 0.0)
        sb = np.where(first_half[None, :], 0.0, sin_l)
        tables.append(np.stack([cos_l, sa, sb]).astype(np.float32))
    return tuple(tables)


@functools.lru_cache(maxsize=None)
def _dft_tables():
    k = np.arange(SEQ, dtype=np.int64)
    n = np.arange(SEQ, dtype=np.int64)
    idx = ((2 * k[:, None] + 1) * n[None, :]) % (2 * N_FFT)
    phi = idx.astype(np.float64) * (2.0 * math.pi / (2 * N_FFT))
    c = np.cos(phi).reshape(N_FREQ_BLOCKS, FREQ_BLOCK, SEQ)
    s = np.sin(phi).reshape(N_FREQ_BLOCKS, FREQ_BLOCK, SEQ)
    fwd = np.concatenate([c, -s], axis=1).reshape(2 * SEQ, SEQ)
    inv = (2.0 / N_FFT) * fwd.T
    return fwd.astype(np.float32), np.ascontiguousarray(inv).astype(np.float32)


@functools.lru_cache(maxsize=None)
def _filter_tables():
    n = np.arange(SEQ, dtype=np.float64)
    t = n / max(SEQ - 1, 1)
    bands = np.linspace(1e-4, FILTER_BANDS - 1, FILTER_BANDS)
    ang = (2.0 * math.pi / SEQ) * n[:, None] * bands[None, :]
    z = np.concatenate([t[:, None], np.cos(ang), -np.sin(ang)], axis=-1)
    z = np.pad(z, ((0, 0), (0, LANES - FILTER_EMB_DIM)))
    max_decay = math.log(DECAY_TARGET) / FAST_DECAY_PCT
    min_decay = math.log(DECAY_TARGET) / SLOW_DECAY_PCT
    deltas = np.abs(np.linspace(min_decay, max_decay, HYENA_WIDTH))
    window = np.exp(-t[:, None] * deltas[None, :])
    return z.astype(np.float32), window.astype(np.float32)


@functools.lru_cache(maxsize=None)
def _attention_masks():
    i = np.arange(Q_BLOCK)[:, None]
    j = np.arange(K_WINDOW)[None, :]
    classes, ids = {}, []
    for d in DILATIONS:
        n = SEQ // d
        for c in range(SEQ // Q_BLOCK):
            q0 = c * Q_BLOCK
            ws = min(max(q0 - RADIUS, 0), SEQ - K_WINDOW)
            seg = (q0 // n) * n
            kpos = ws + j
            valid = (np.abs(kpos - (q0 + i)) <= RADIUS) & (kpos >= seg) & (kpos < seg + n)
            bias = np.where(valid, 0.0, NEG_INF).astype(np.float32)
            ids.append(classes.setdefault(bias.tobytes(), len(classes)))
    table = np.stack([np.frombuffer(b, np.float32).reshape(Q_BLOCK, K_WINDOW) for b in classes])
    return table, np.asarray(ids, np.int32)


def _rms(x, gain):
    return x * lax.rsqrt(jnp.mean(x * x, axis=-1, keepdims=True) + RMS_EPS) * gain


def _store_row_orders(hn, o0_ref, o1_ref, o2_ref, a_ref, b_ref):
    d1, d2 = DILATIONS[1], DILATIONS[2]
    n1, n2 = ROW_TILE // d1, ROW_TILE // d2
    o0_ref[...] = hn.astype(BF16)
    for s in range(N_SLABS):
        a_ref[s] = hn[:, s * LANES:(s + 1) * LANES]
    for r in range(d1):
        v = jnp.concatenate([a_ref[s, pl.ds(r, n1, stride=d1), :] for s in range(N_SLABS)], axis=1)
        o1_ref[r] = v.astype(BF16)
        for s in range(N_SLABS):
            b_ref[s, r * n1:(r + 1) * n1, :] = v[:, s * LANES:(s + 1) * LANES]
    for r in range(d2):
        src = pl.ds((r % d1) * n1 + r // d1, n2, stride=d1)
        v = jnp.concatenate([b_ref[s, src, :] for s in range(N_SLABS)], axis=1)
        o2_ref[r] = v.astype(BF16)


def _row_order_outputs(batch):
    d1, d2 = DILATIONS[1], DILATIONS[2]
    tiles = SEQ // ROW_TILE
    shapes = [jax.ShapeDtypeStruct((batch * SEQ, D_MODEL), BF16),
              jax.ShapeDtypeStruct((batch, d1, SEQ // d1, D_MODEL), BF16),
              jax.ShapeDtypeStruct((batch, d2, SEQ // d2, D_MODEL), BF16)]
    specs = [pl.BlockSpec((ROW_TILE, D_MODEL), lambda i: (i, 0)),
             pl.BlockSpec((None, d1, ROW_TILE // d1, D_MODEL), lambda i: (i // tiles, 0, i % tiles, 0)),
             pl.BlockSpec((None, d2, ROW_TILE // d2, D_MODEL), lambda i: (i // tiles, 0, i % tiles, 0))]
    scratch = [pltpu.VMEM((N_SLABS, ROW_TILE, LANES), F32), pltpu.VMEM((N_SLABS, ROW_TILE, LANES), F32)]
    return shapes, specs, scratch


def _norm_perm_kernel(x_ref, g_ref, o0_ref, o1_ref, o2_ref, a_ref, b_ref):
    _store_row_orders(_rms(x_ref[...], g_ref[...]), o0_ref, o1_ref, o2_ref, a_ref, b_ref)


def _norm_perm(x2d, gain, batch):
    shapes, specs, scratch = _row_order_outputs(batch)
    return pl.pallas_call(
        _norm_perm_kernel,
        grid=(batch * SEQ // ROW_TILE,),
        in_specs=[pl.BlockSpec((ROW_TILE, D_MODEL), lambda i: (i, 0)), pl.BlockSpec((1, D_MODEL), lambda i: (0, 0))],
        out_specs=specs,
        out_shape=shapes,
        scratch_shapes=scratch,
        compiler_params=_params(1),
        name="norm_perm",
    )(x2d, gain.reshape(1, D_MODEL))


def _filter_kernel(z_ref, win_ref, w1_ref, b1_ref, wi_ref, bi_ref, wo_ref, fr_ref, f_ref,
                   kr_ref, ki_ref, hcat_ref):
    i = pl.program_id(0)
    rows_per_chunk = 256
    hi = lax.Precision.HIGHEST

    @pl.when(i == 0)
    def _():
        fr = fr_ref[...]

        def chunk(c, carry):
            r0 = pl.multiple_of(c * rows_per_chunk, rows_per_chunk)
            rows = pl.ds(r0, rows_per_chunk)
            hid = jnp.sin(fr * (jnp.dot(z_ref[rows, :], w1_ref[...], precision=hi,
                                        preferred_element_type=F32) + b1_ref[...]))
            for l in range(FILTER_INNER):
                hid = jnp.sin(fr * (jnp.dot(hid, wi_ref[l], precision=hi,
                                            preferred_element_type=F32) + bi_ref[l]))
            filt = jnp.dot(hid, wo_ref[...], precision=hi, preferred_element_type=F32)
            win = win_ref[rows, :]
            row = r0 + lax.broadcasted_iota(jnp.int32, (rows_per_chunk, 1), 0)
            h_fwd = filt[:, :HYENA_WIDTH] * win
            h_bwd = jnp.where(row == 0, 0.0, filt[:, HYENA_WIDTH:] * win)
            hcat_ref[rows, :HYENA_WIDTH] = h_fwd.astype(BF16)
            hcat_ref[rows, HYENA_WIDTH:] = h_bwd.astype(BF16)
            return carry

        lax.fori_loop(0, SEQ // rows_per_chunk, chunk, 0)

    sp = jnp.dot(f_ref[...], hcat_ref[...], preferred_element_type=F32)
    re, im = sp[:FREQ_BLOCK], sp[FREQ_BLOCK:]
    kr_ref[...] = re[:, :HYENA_WIDTH] + re[:, HYENA_WIDTH:]
    ki_ref[...] = im[:, :HYENA_WIDTH] - im[:, HYENA_WIDTH:]


def _hyena_filter_spectrum(z, window, w1, b1, w_inner, b_inner, w_out, freq, dft_fwd):
    full = lambda a: pl.BlockSpec(a.shape, lambda i: (0,) * a.ndim)
    w1p = jnp.pad(w1, ((0, LANES - FILTER_EMB_DIM), (0, 0)))
    b1 = b1.reshape(1, FILTER_HIDDEN)
    b_inner = b_inner.reshape(FILTER_INNER, 1, FILTER_HIDDEN)
    freq = freq.reshape(1, FILTER_HIDDEN)
    ops = (z, window, w1p, b1, w_inner, b_inner, w_out, freq)
    out_spec = pl.BlockSpec((FREQ_BLOCK, HYENA_WIDTH), lambda i: (i, 0))
    return pl.pallas_call(
        _filter_kernel,
        grid=(N_FREQ_BLOCKS,),
        in_specs=[full(a) for a in ops] + [pl.BlockSpec((2 * FREQ_BLOCK, SEQ), lambda i: (i, 0))],
        out_specs=[out_spec, out_spec],
        out_shape=[jax.ShapeDtypeStruct((SEQ, HYENA_WIDTH), F32)] * 2,
        scratch_shapes=[pltpu.VMEM((SEQ, 2 * HYENA_WIDTH), BF16)],
        compiler_params=_params(1),
        name="hyena_filter",
    )(*ops, dft_fwd)


SUBLANES = 8


def _hy_proj_kernel(hn_ref, w_ref, cw_ref, cb_ref, x0_ref, uv_ref):
    n_chunks = SEQ // ROW_CHUNK
    pre = [jnp.dot(hn_ref[rc * ROW_CHUNK:(rc + 1) * ROW_CHUNK, :], w_ref[...], preferred_element_type=F32)
           for rc in range(n_chunks)]
    w = cw_ref[...]
    bias = cb_ref[...]
    zeros = jnp.zeros((SUBLANES, HY_IN_WIDTH), F32)
    for rc in range(n_chunks):
        before = pre[rc - 1][ROW_CHUNK - SUBLANES:] if rc > 0 else zeros
        after = pre[rc + 1][:SUBLANES] if rc < n_chunks - 1 else zeros
        ext = ROW_CHUNK + SUBLANES
        prev = pltpu.roll(jnp.concatenate([before, pre[rc]], axis=0), 1, 0)[SUBLANES:]
        nxt = pltpu.roll(jnp.concatenate([pre[rc], after], axis=0), ext - 1, 0)[:ROW_CHUNK]
        u = bias + prev * w[0:1] + pre[rc] * w[1:2] + nxt * w[2:3]
        rows = slice(rc * ROW_CHUNK, (rc + 1) * ROW_CHUNK)
        x0_ref[rows, :] = u[:, :HYENA_WIDTH].astype(BF16)
        uv_ref[rows, :] = (u[:, HYENA_WIDTH:2 * HYENA_WIDTH] * u[:, 2 * HYENA_WIDTH:]).astype(BF16)


def _hy_proj(hn0, w_hy, conv_w, conv_b, batch):
    const = lambda a: pl.BlockSpec(a.shape, lambda b: (0,) * a.ndim)
    conv_b = conv_b.reshape(1, HY_IN_WIDTH)
    out_spec = pl.BlockSpec((SEQ, HYENA_WIDTH), lambda b: (b, 0))
    out_shape = jax.ShapeDtypeStruct((batch * SEQ, HYENA_WIDTH), BF16)
    return pl.pallas_call(
        _hy_proj_kernel,
        grid=(batch,),
        in_specs=[pl.BlockSpec((SEQ, D_MODEL), lambda b: (b, 0)), const(w_hy), const(conv_w), const(conv_b)],
        out_specs=[out_spec, out_spec],
        out_shape=[out_shape, out_shape],
        compiler_params=_params(1),
        name="hy_proj",
    )(hn0, w_hy, conv_w, conv_b)


def _qkv_proj_kernel(hn_ref, w_ref, rope_ref, o_ref):
    half = HEAD_DIM // 2
    q_scale = HEAD_DIM ** -0.5 * math.log2(math.e)
    for rc in range(SEQ // ROW_CHUNK):
        rows = slice(rc * ROW_CHUNK, (rc + 1) * ROW_CHUNK)
        acc = jnp.dot(hn_ref[rows, :], w_ref[...], preferred_element_type=F32)
        cos, sa, sb = rope_ref[0, rows, :], rope_ref[1, rows, :], rope_ref[2, rows, :]
        for c in range(2 * ATTN_WIDTH // LANES):
            t = acc[:, c * LANES:(c + 1) * LANES]
            if c < ATTN_WIDTH // LANES:
                t = t * q_scale
            out = t * cos + pltpu.roll(t, LANES - half, 1) * sa + pltpu.roll(t, half, 1) * sb
            o_ref[rows, c * LANES:(c + 1) * LANES] = out.astype(BF16)
        o_ref[rows, 2 * ATTN_WIDTH:] = acc[:, 2 * ATTN_WIDTH:].astype(BF16)


def _qkv_proj(hn, w_qkv, rope, batch):
    const = lambda a: pl.BlockSpec(a.shape, lambda b: (0,) * a.ndim)
    return pl.pallas_call(
        _qkv_proj_kernel,
        grid=(batch,),
        in_specs=[pl.BlockSpec((SEQ, D_MODEL), lambda b: (b, 0)), const(w_qkv), const(rope)],
        out_specs=pl.BlockSpec((SEQ, GROUP_WIDTH), lambda b: (b, 0)),
        out_shape=jax.ShapeDtypeStruct((batch * SEQ, GROUP_WIDTH), BF16),
        compiler_params=_params(1),
        name="qkv_proj",
    )(hn, w_qkv, rope)


def _hyena_kernel(x0_ref, uv_ref, skip_ref, f_ref, g_ref, kr_ref, ki_ref, o_ref, acc_ref):
    i = pl.program_id(1)

    @pl.when(i == 0)
    def _():
        acc_ref[...] = uv_ref[...].astype(F32) * skip_ref[...]

    sp = jnp.dot(f_ref[...], uv_ref[...], preferred_element_type=F32)
    re, im = sp[:FREQ_BLOCK], sp[FREQ_BLOCK:]
    kr, ki = kr_ref[...], ki_ref[...]
    prod = jnp.concatenate([re * kr - im * ki, re * ki + im * kr], axis=0).astype(BF16)
    acc_ref[...] += jnp.dot(g_ref[...], prod, preferred_element_type=F32)

    @pl.when(i == N_FREQ_BLOCKS - 1)
    def _():
        o_ref[...] = (x0_ref[...].astype(F32) * acc_ref[...]).astype(BF16)


def _hyena(x0c, uv, skip, dft_fwd, dft_inv, kr, ki, batch):
    slab = pl.BlockSpec((SEQ, HYENA_WIDTH), lambda b, i: (b, 0))
    skip = skip.reshape(1, HYENA_WIDTH)
    kspec = pl.BlockSpec((FREQ_BLOCK, HYENA_WIDTH), lambda b, i: (i, 0))
    return pl.pallas_call(
        _hyena_kernel,
        grid=(batch, N_FREQ_BLOCKS),
        in_specs=[slab, slab, pl.BlockSpec((1, HYENA_WIDTH), lambda b, i: (0, 0)),
                  pl.BlockSpec((2 * FREQ_BLOCK, SEQ), lambda b, i: (i, 0)),
                  pl.BlockSpec((SEQ, 2 * FREQ_BLOCK), lambda b, i: (0, i)),
                  kspec, kspec],
        out_specs=slab,
        out_shape=jax.ShapeDtypeStruct((batch * SEQ, HYENA_WIDTH), BF16),
        scratch_shapes=[pltpu.VMEM((SEQ, HYENA_WIDTH), F32)],
        compiler_params=_params(2),
        name="hyena",
    )(x0c, uv, skip, dft_fwd, dft_inv, kr, ki)


def _attention_kernel(cls_ref, bias_ref, *refs):
    qkv_refs = refs[:3 * N_GROUPS]
    o_ref, acc_ref, den_ref, max_ref = refs[3 * N_GROUPS:]
    lane = lax.broadcasted_iota(jnp.int32, (1, LANES), 1)
    first = lane < HEAD_DIM
    ones = jnp.ones((K_WINDOW, LANES), BF16)
    n_blocks = SEQ // Q_BLOCK

    for g, d in enumerate(DILATIONS):
        q_ref, k_ref, v_ref = qkv_refs[3 * g:3 * g + 3]
        n = SEQ // d

        def q_block(c, carry, g=g, d=d, n=n, q_ref=q_ref, k_ref=k_ref, v_ref=v_ref):
            q0 = pl.multiple_of(c * Q_BLOCK, Q_BLOCK)
            ws = pl.multiple_of(jnp.clip(q0 - RADIUS, 0, SEQ - K_WINDOW), RADIUS)
            q = q_ref[pl.ds(q0, Q_BLOCK), :]
            zero = jnp.zeros_like(q)
            q2 = jnp.concatenate([jnp.where(first, q, zero), jnp.where(first, zero, q)], axis=0)
            s = lax.dot_general(q2, k_ref[pl.ds(ws, K_WINDOW), :], (((1,), (1,)), ((), ())),
                                preferred_element_type=F32)
            bias = bias_ref[cls_ref[g * n_blocks + c]]
            s = s + jnp.concatenate([bias, bias], axis=0)
            m = jnp.max(s, axis=1, keepdims=True)
            p = jnp.exp2(s - m).astype(BF16)
            out = jnp.dot(p, jnp.concatenate([v_ref[pl.ds(ws, K_WINDOW), :], ones], axis=1),
                          preferred_element_type=F32)
            if d == 1:
                dst = pl.ds(q0, Q_BLOCK)
            else:
                dst = pl.ds(q0 // n + d * lax.rem(q0, n), Q_BLOCK, stride=d)
            acc_ref[g, dst, :] = jnp.where(first, out[:Q_BLOCK, :LANES], out[Q_BLOCK:, :LANES])
            den_ref[g, dst, :] = jnp.where(first, out[:Q_BLOCK, LANES:], out[Q_BLOCK:, LANES:])
            max_ref[g, dst, :] = jnp.where(first, m[:Q_BLOCK], m[Q_BLOCK:])
            return carry

        lax.fori_loop(0, n_blocks, q_block, 0, unroll=n_blocks)

    def combine(c, carry):
        rows = pl.ds(pl.multiple_of(c * Q_BLOCK, Q_BLOCK), Q_BLOCK)
        ms = [max_ref[g, rows, :] for g in range(N_GROUPS)]
        top = functools.reduce(jnp.maximum, ms)
        ws = [jnp.exp2(m - top) for m in ms]
        num = sum(ws[g] * acc_ref[g, rows, :] for g in range(N_GROUPS))
        den = sum(ws[g] * den_ref[g, rows, :] for g in range(N_GROUPS))
        o_ref[rows, :] = (num / den).astype(BF16)
        return carry

    lax.fori_loop(0, n_blocks, combine, 0, unroll=2)


def _attention(qkvs, batch):
    pairs = ATTN_WIDTH // LANES
    table, ids = (jnp.asarray(a) for a in _attention_masks())
    specs = [pl.BlockSpec(memory_space=pltpu.SMEM),
             pl.BlockSpec(table.shape, lambda b, hp: (0, 0, 0))]
    operands = []
    for g in range(N_GROUPS):
        for t in range(3):
            specs.append(pl.BlockSpec((SEQ, LANES), functools.partial(lambda b, hp, t: (b, t * pairs + hp), t=t)))
            operands.append(qkvs[g])
    group_rows = pltpu.VMEM((N_GROUPS, SEQ, LANES), F32)
    return pl.pallas_call(
        _attention_kernel,
        grid=(batch, pairs),
        in_specs=specs,
        out_specs=pl.BlockSpec((SEQ, LANES), lambda b, hp: (b, hp)),
        out_shape=jax.ShapeDtypeStruct((batch * SEQ, ATTN_WIDTH), BF16),
        scratch_shapes=[group_rows, group_rows, group_rows],
        compiler_params=_params(2),
        name="attention",
    )(ids, table, *operands)


def _merge_kernel(x_ref, hn_ref, yh_ref, ya_ref, wg_ref, ph_ref, pa_ref, wo_ref, o_ref):
    hn = hn_ref[...]
    g_hy = jax.nn.sigmoid(jnp.dot(hn, wg_ref[:, :D_MODEL], preferred_element_type=F32))
    merged = g_hy * jnp.dot(yh_ref[...], ph_ref[...], preferred_element_type=F32)
    g_att = jax.nn.sigmoid(jnp.dot(hn, wg_ref[:, D_MODEL:], preferred_element_type=F32))
    merged = merged + g_att * jnp.dot(ya_ref[...], pa_ref[...], preferred_element_type=F32)
    o_ref[...] = x_ref[...] + jnp.dot(merged.astype(BF16), wo_ref[...], preferred_element_type=F32)


def _merge(x2d, hn0, y_hy, y_att, w_gate, p_hy, p_att, w_o, rows):
    tm = 1024
    row = lambda w: pl.BlockSpec((tm, w), lambda i: (i, 0))
    const = lambda a: pl.BlockSpec(a.shape, lambda i: (0, 0))
    return pl.pallas_call(
        _merge_kernel,
        grid=(rows // tm,),
        in_specs=[row(D_MODEL), row(D_MODEL), row(HYENA_WIDTH), row(ATTN_WIDTH),
                  const(w_gate), const(p_hy), const(p_att), const(w_o)],
        out_specs=row(D_MODEL),
        out_shape=jax.ShapeDtypeStruct((rows, D_MODEL), F32),
        compiler_params=_params(1),
        name="merge",
    )(x2d, hn0, y_hy, y_att, w_gate, p_hy, p_att, w_o)


def _ffn_kernel(x_ref, g_ref, w1_ref, w2_ref, gn_ref, *rest, last):
    x = x_ref[...]
    hn = _rms(x, g_ref[...]).astype(BF16)
    chunk = 1024
    y = x
    for c in range(D_FF // chunk):
        h = jnp.dot(hn, w1_ref[:, c * chunk:(c + 1) * chunk], preferred_element_type=F32)
        h = jnp.square(jnp.maximum(h, 0.0)).astype(BF16)
        y = y + jnp.dot(h, w2_ref[c * chunk:(c + 1) * chunk, :], preferred_element_type=F32)
    if last:
        (o_ref,) = rest
        o_ref[...] = _rms(y, gn_ref[...])
    else:
        o_ref, o0_ref, o1_ref, o2_ref, a_ref, b_ref = rest
        o_ref[...] = y
        _store_row_orders(_rms(y, gn_ref[...]), o0_ref, o1_ref, o2_ref, a_ref, b_ref)


def _ffn(x2d, gain, w1, w2, gain_next, batch, last):
    rows = batch * SEQ
    row = pl.BlockSpec((ROW_TILE, D_MODEL), lambda i: (i, 0))
    const = lambda a: pl.BlockSpec(a.shape, lambda i: (0, 0))
    gain = gain.reshape(1, D_MODEL)
    gain_next = gain_next.reshape(1, D_MODEL)
    out_shape = [jax.ShapeDtypeStruct((rows, D_MODEL), F32)]
    out_specs = [row]
    scratch = []
    if not last:
        shapes, specs, scratch = _row_order_outputs(batch)
        out_shape += shapes
        out_specs += specs
    return pl.pallas_call(
        functools.partial(_ffn_kernel, last=last),
        grid=(rows // ROW_TILE,),
        in_specs=[row, const(gain), const(w1), const(w2), const(gain_next)],
        out_specs=out_specs,
        out_shape=out_shape,
        scratch_shapes=scratch,
        compiler_params=_params(1),
        name="ffn",
    )(x2d, gain, w1, w2, gain_next)


def kernel(x, norm_mix, w_in, conv_w, conv_b, filt_w1, filt_b1, filt_w_inner, filt_b_inner, filt_w_out,
           filt_freq, hy_skip, p_hy, p_att, w_o, norm_ffn, w_ff1, w_ff2, norm_final):
    batch, seq, d_model = x.shape
    assert (seq, d_model) == (SEQ, D_MODEL)
    depth = norm_mix.shape[0]
    rows = batch * seq
    rope = tuple(jnp.asarray(t) for t in _rope_tables())
    dft_fwd, dft_inv = (jnp.asarray(t).astype(BF16) for t in _dft_tables())
    z, window = (jnp.asarray(t) for t in _filter_tables())
    bf = lambda a: a.astype(BF16)

    x2d = x.reshape(rows, d_model)
    hns = _norm_perm(x2d, norm_mix[0], batch)
    for l in range(depth):
        hns = [h.reshape(rows, d_model) for h in hns]
        kr, ki = _hyena_filter_spectrum(z, window, filt_w1[l], filt_b1[l], filt_w_inner[l], filt_b_inner[l],
                                        filt_w_out[l], filt_freq[l], dft_fwd)
        x0c, uv = _hy_proj(hns[0], bf(w_in[l, :, :HY_IN_WIDTH]), conv_w[l], conv_b[l], batch)
        y_hy = _hyena(x0c, uv, hy_skip[l], dft_fwd, dft_inv, kr, ki, batch)
        qkvs = [_qkv_proj(hns[g], bf(w_in[l, :, HY_IN_WIDTH + g * GROUP_WIDTH:HY_IN_WIDTH + (g + 1) * GROUP_WIDTH]),
                          rope[g], batch) for g in range(N_GROUPS)]
        y_att = _attention(qkvs, batch)
        x2d = _merge(x2d, hns[0], y_hy, y_att, bf(w_in[l, :, HY_IN_WIDTH + QKV_WIDTH:]), bf(p_hy[l]), bf(p_att[l]),
                     bf(w_o[l]), rows)
        last = l == depth - 1
        outs = _ffn(x2d, norm_ffn[l], bf(w_ff1[l]), bf(w_ff2[l]), norm_final if last else norm_mix[l + 1], batch, last)
        x2d, hns = outs[0], outs[1:]
    return x2d.reshape(batch, seq, d_model)
```

```python
import functools
import math

import numpy as np
import jax
import jax.numpy as jnp
from jax import lax
from jax.experimental import pallas as pl
from jax.experimental.pallas import tpu as pltpu

F32 = jnp.float32
BF16 = jnp.bfloat16

D_MODEL = 1024
SEQ = 2048
HEAD_DIM = 64
ATTN_HEADS = 8
DILATIONS = (1, 4, 16)
RADIUS = 64
N_GROUPS = len(DILATIONS)
ATTN_WIDTH = ATTN_HEADS * HEAD_DIM
ROPE_THETA = 10000.0
NEG_INF = -1e30
HYENA_WIDTH = D_MODEL // 2
FILTER_BANDS = 16
FILTER_EMB_DIM = 1 + 2 * FILTER_BANDS
FILTER_HIDDEN = 64
FILTER_INNER = 2
DECAY_TARGET = 1e-2
FAST_DECAY_PCT = 0.3
SLOW_DECAY_PCT = 1.5
HY_IN_WIDTH = 3 * HYENA_WIDTH
GROUP_WIDTH = 3 * ATTN_WIDTH
QKV_WIDTH = N_GROUPS * GROUP_WIDTH
D_FF = 4 * D_MODEL
RMS_EPS = 1e-6

LANES = 128
N_SLABS = D_MODEL // LANES
N_FFT = 2 * SEQ
FREQ_BLOCK = 512
N_FREQ_BLOCKS = SEQ // FREQ_BLOCK
Q_BLOCK = 128
K_WINDOW = 256
PAIRS_PER_STEP = 1
ROW_TILE = 512
ROW_CHUNK = 512
VMEM_LIMIT = 56 * 1024 * 1024


def _params(n_axes, vmem=VMEM_LIMIT):
    return pltpu.CompilerParams(dimension_semantics=("arbitrary",) * n_axes, vmem_limit_bytes=vmem)


def _perm_positions(dilation):
    n = SEQ // dilation
    r = np.arange(SEQ) // n
    m = np.arange(SEQ) % n
    return m * dilation + r


@functools.lru_cache(maxsize=None)
def _rope_tables():
    half = HEAD_DIM // 2
    inv_freq = ROPE_THETA ** (-np.arange(half, dtype=np.float64) / half)
    lane = np.arange(LANES)
    first_half = (lane % HEAD_DIM) < half
    tables = []
    for d in DILATIONS:
        ang = _perm_positions(d)[:, None].astype(np.float64) * inv_freq[None, :]
        cos_l = np.cos(ang)[:, lane % half]
        sin_l = np.sin(ang)[:, lane % half]
        sa = np.where(first_half[None, :], -sin_l, 0.0)
        sb = np.where(first_half[None, :], 0.0, sin_l)
        tables.append(np.stack([cos_l, sa, sb]).astype(np.float32))
    return tuple(tables)


@functools.lru_cache(maxsize=None)
def _dft_tables():
    k = np.arange(SEQ, dtype=np.int64)
    n = np.arange(SEQ, dtype=np.int64)
    idx = ((2 * k[:, None] + 1) * n[None, :]) % (2 * N_FFT)
    phi = idx.astype(np.float64) * (2.0 * math.pi / (2 * N_FFT))
    c = np.cos(phi).reshape(N_FREQ_BLOCKS, FREQ_BLOCK, SEQ)
    s = np.sin(phi).reshape(N_FREQ_BLOCKS, FREQ_BLOCK, SEQ)
    fwd = np.concatenate([c, -s], axis=1).reshape(2 * SEQ, SEQ)
    inv = (2.0 / N_FFT) * fwd.T
    return fwd.astype(np.float32), np.ascontiguousarray(inv).astype(np.float32)


@functools.lru_cache(maxsize=None)
def _filter_tables():
    n = np.arange(SEQ, dtype=np.float64)
    t = n / max(SEQ - 1, 1)
    bands = np.linspace(1e-4, FILTER_BANDS - 1, FILTER_BANDS)
    ang = (2.0 * math.pi / SEQ) * n[:, None] * bands[None, :]
    z = np.concatenate([t[:, None], np.cos(ang), -np.sin(ang)], axis=-1)
    z = np.pad(z, ((0, 0), (0, LANES - FILTER_EMB_DIM)))
    max_decay = math.log(DECAY_TARGET) / FAST_DECAY_PCT
    min_decay = math.log(DECAY_TARGET) / SLOW_DECAY_PCT
    deltas = np.abs(np.linspace(min_decay, max_decay, HYENA_WIDTH))
    window = np.exp(-t[:, None] * deltas[None, :])
    return z.astype(np.float32), window.astype(np.float32)


@functools.lru_cache(maxsize=None)
def _attention_masks():
    i = np.arange(Q_BLOCK)[:, None]
    j = np.arange(K_WINDOW)[None, :]
    classes, ids = {}, []
    for d in DILATIONS:
        n = SEQ // d
        for c in range(SEQ // Q_BLOCK):
            q0 = c * Q_BLOCK
            ws = min(max(q0 - RADIUS, 0), SEQ - K_WINDOW)
            seg = (q0 // n) * n
            kpos = ws + j
            valid = (np.abs(kpos - (q0 + i)) <= RADIUS) & (kpos >= seg) & (kpos < seg + n)
            bias = np.where(valid, 0.0, NEG_INF).astype(np.float32)
            ids.append(classes.setdefault(bias.tobytes(), len(classes)))
    table = np.stack([np.frombuffer(b, np.float32).reshape(Q_BLOCK, K_WINDOW) for b in classes])
    return table, np.asarray(ids, np.int32)


def _rms(x, gain):
    return x * lax.rsqrt(jnp.mean(x * x, axis=-1, keepdims=True) + RMS_EPS) * gain


def _norm_kernel(x_ref, g_ref, o_ref):
    o_ref[...] = _rms(x_ref[...], g_ref[...]).astype(BF16)


def _norm(x2d, gain):
    rows = x2d.shape[0]
    tile = 1024
    row = pl.BlockSpec((tile, D_MODEL), lambda i: (i, 0))
    return pl.pallas_call(
        _norm_kernel,
        grid=(rows // tile,),
        in_specs=[row, pl.BlockSpec((1, D_MODEL), lambda i: (0, 0))],
        out_specs=row,
        out_shape=jax.ShapeDtypeStruct((rows, D_MODEL), BF16),
        compiler_params=_params(1),
        name="norm",
    )(x2d, gain.reshape(1, D_MODEL))


def _filter_kernel(z_ref, win_ref, w1_ref, b1_ref, wi_ref, bi_ref, wo_ref, fr_ref, f_ref,
                   kr_ref, ki_ref, hcat_ref):
    i = pl.program_id(0)
    rows_per_chunk = 256
    hi = lax.Precision.HIGHEST

    @pl.when(i == 0)
    def _():
        fr = fr_ref[...]

        def chunk(c, carry):
            r0 = pl.multiple_of(c * rows_per_chunk, rows_per_chunk)
            rows = pl.ds(r0, rows_per_chunk)
            hid = jnp.sin(fr * (jnp.dot(z_ref[rows, :], w1_ref[...], precision=hi,
                                        preferred_element_type=F32) + b1_ref[...]))
            for l in range(FILTER_INNER):
                hid = jnp.sin(fr * (jnp.dot(hid, wi_ref[l], precision=hi,
                                            preferred_element_type=F32) + bi_ref[l]))
            filt = jnp.dot(hid, wo_ref[...], precision=hi, preferred_element_type=F32)
            win = win_ref[rows, :]
            row = r0 + lax.broadcasted_iota(jnp.int32, (rows_per_chunk, 1), 0)
            h_fwd = filt[:, :HYENA_WIDTH] * win
            h_bwd = jnp.where(row == 0, 0.0, filt[:, HYENA_WIDTH:] * win)
            hcat_ref[rows, :HYENA_WIDTH] = h_fwd.astype(BF16)
            hcat_ref[rows, HYENA_WIDTH:] = h_bwd.astype(BF16)
            return carry

        lax.fori_loop(0, SEQ // rows_per_chunk, chunk, 0)

    sp = jnp.dot(f_ref[...], hcat_ref[...], preferred_element_type=F32)
    re, im = sp[:FREQ_BLOCK], sp[FREQ_BLOCK:]
    kr_ref[...] = re[:, :HYENA_WIDTH] + re[:, HYENA_WIDTH:]
    ki_ref[...] = im[:, :HYENA_WIDTH] - im[:, HYENA_WIDTH:]


def _hyena_filter_spectrum(z, window, w1, b1, w_inner, b_inner, w_out, freq, dft_fwd):
    full = lambda a: pl.BlockSpec(a.shape, lambda i: (0,) * a.ndim)
    w1p = jnp.pad(w1, ((0, LANES - FILTER_EMB_DIM), (0, 0)))
    b1 = b1.reshape(1, FILTER_HIDDEN)
    b_inner = b_inner.reshape(FILTER_INNER, 1, FILTER_HIDDEN)
    freq = freq.reshape(1, FILTER_HIDDEN)
    ops = (z, window, w1p, b1, w_inner, b_inner, w_out, freq)
    out_spec = pl.BlockSpec((FREQ_BLOCK, HYENA_WIDTH), lambda i: (i, 0))
    return pl.pallas_call(
        _filter_kernel,
        grid=(N_FREQ_BLOCKS,),
        in_specs=[full(a) for a in ops] + [pl.BlockSpec((2 * FREQ_BLOCK, SEQ), lambda i: (i, 0))],
        out_specs=[out_spec, out_spec],
        out_shape=[jax.ShapeDtypeStruct((SEQ, HYENA_WIDTH), F32)] * 2,
        scratch_shapes=[pltpu.VMEM((SEQ, 2 * HYENA_WIDTH), BF16)],
        compiler_params=_params(1),
        name="hyena_filter",
    )(*ops, dft_fwd)


SUBLANES = 8


def _hy_proj_kernel(hn_ref, w_ref, cw_ref, cb_ref, x0_ref, uv_ref):
    n_chunks = SEQ // ROW_CHUNK
    pre = [jnp.dot(hn_ref[rc * ROW_CHUNK:(rc + 1) * ROW_CHUNK, :], w_ref[...], preferred_element_type=F32)
           for rc in range(n_chunks)]
    w = cw_ref[...]
    bias = cb_ref[...]
    zeros = jnp.zeros((SUBLANES, HY_IN_WIDTH), F32)
    for rc in range(n_chunks):
        before = pre[rc - 1][ROW_CHUNK - SUBLANES:] if rc > 0 else zeros
        after = pre[rc + 1][:SUBLANES] if rc < n_chunks - 1 else zeros
        ext = ROW_CHUNK + SUBLANES
        prev = pltpu.roll(jnp.concatenate([before, pre[rc]], axis=0), 1, 0)[SUBLANES:]
        nxt = pltpu.roll(jnp.concatenate([pre[rc], after], axis=0), ext - 1, 0)[:ROW_CHUNK]
        u = bias + prev * w[0:1] + pre[rc] * w[1:2] + nxt * w[2:3]
        rows = slice(rc * ROW_CHUNK, (rc + 1) * ROW_CHUNK)
        x0_ref[rows, :] = u[:, :HYENA_WIDTH].astype(BF16)
        uv_ref[rows, :] = (u[:, HYENA_WIDTH:2 * HYENA_WIDTH] * u[:, 2 * HYENA_WIDTH:]).astype(BF16)


def _hy_proj(hn0, w_hy, conv_w, conv_b, batch):
    const = lambda a: pl.BlockSpec(a.shape, lambda b: (0,) * a.ndim)
    conv_b = conv_b.reshape(1, HY_IN_WIDTH)
    out_spec = pl.BlockSpec((SEQ, HYENA_WIDTH), lambda b: (b, 0))
    out_shape = jax.ShapeDtypeStruct((batch * SEQ, HYENA_WIDTH), BF16)
    return pl.pallas_call(
        _hy_proj_kernel,
        grid=(batch,),
        in_specs=[pl.BlockSpec((SEQ, D_MODEL), lambda b: (b, 0)), const(w_hy), const(conv_w), const(conv_b)],
        out_specs=[out_spec, out_spec],
        out_shape=[out_shape, out_shape],
        compiler_params=_params(1),
        name="hy_proj",
    )(hn0, w_hy, conv_w, conv_b)


def _qkv_proj_kernel(hn_ref, w_ref, rope_ref, o_ref, *, dilation):
    n = SEQ // dilation
    chunk = min(n, ROW_CHUNK)
    half = HEAD_DIM // 2
    q_scale = HEAD_DIM ** -0.5 * math.log2(math.e)
    for r in range(dilation):
        for rc in range(n // chunk):
            src = slice(rc * chunk, (rc + 1) * chunk)
            rows = slice(r * n + rc * chunk, r * n + (rc + 1) * chunk)
            acc = jnp.dot(hn_ref[src, r * D_MODEL:(r + 1) * D_MODEL], w_ref[...],
                          preferred_element_type=F32)
            cos, sa, sb = rope_ref[0, rows, :], rope_ref[1, rows, :], rope_ref[2, rows, :]
            for c in range(2 * ATTN_WIDTH // LANES):
                t = acc[:, c * LANES:(c + 1) * LANES]
                if c < ATTN_WIDTH // LANES:
                    t = t * q_scale
                out = t * cos + pltpu.roll(t, LANES - half, 1) * sa + pltpu.roll(t, half, 1) * sb
                o_ref[rows, c * LANES:(c + 1) * LANES] = out.astype(BF16)
            o_ref[rows, 2 * ATTN_WIDTH:] = acc[:, 2 * ATTN_WIDTH:].astype(BF16)


def _qkv_proj(hn, w_qkv, rope, batch, dilation):
    const = lambda a: pl.BlockSpec(a.shape, lambda b: (0,) * a.ndim)
    n = SEQ // dilation
    return pl.pallas_call(
        functools.partial(_qkv_proj_kernel, dilation=dilation),
        grid=(batch,),
        in_specs=[pl.BlockSpec((n, dilation * D_MODEL), lambda b: (b, 0)), const(w_qkv), const(rope)],
        out_specs=pl.BlockSpec((SEQ, GROUP_WIDTH), lambda b: (b, 0)),
        out_shape=jax.ShapeDtypeStruct((batch * SEQ, GROUP_WIDTH), BF16),
        compiler_params=_params(1),
        name="qkv_proj",
    )(hn.reshape(batch * n, dilation * D_MODEL), w_qkv, rope)


def _hyena_kernel(x0_ref, uv_ref, skip_ref, f_ref, g_ref, kr_ref, ki_ref, o_ref, acc_ref):
    i = pl.program_id(1)

    @pl.when(i == 0)
    def _():
        acc_ref[...] = uv_ref[...].astype(F32) * skip_ref[...]

    sp = jnp.dot(f_ref[...], uv_ref[...], preferred_element_type=F32)
    re, im = sp[:FREQ_BLOCK], sp[FREQ_BLOCK:]
    kr, ki = kr_ref[...], ki_ref[...]
    prod = jnp.concatenate([re * kr - im * ki, re * ki + im * kr], axis=0).astype(BF16)
    acc_ref[...] += jnp.dot(g_ref[...], prod, preferred_element_type=F32)

    @pl.when(i == N_FREQ_BLOCKS - 1)
    def _():
        o_ref[...] = (x0_ref[...].astype(F32) * acc_ref[...]).astype(BF16)


def _hyena(x0c, uv, skip, dft_fwd, dft_inv, kr, ki, batch):
    slab = pl.BlockSpec((SEQ, HYENA_WIDTH), lambda b, i: (b, 0))
    skip = skip.reshape(1, HYENA_WIDTH)
    kspec = pl.BlockSpec((FREQ_BLOCK, HYENA_WIDTH), lambda b, i: (i, 0))
    return pl.pallas_call(
        _hyena_kernel,
        grid=(batch, N_FREQ_BLOCKS),
        in_specs=[slab, slab, pl.BlockSpec((1, HYENA_WIDTH), lambda b, i: (0, 0)),
                  pl.BlockSpec((2 * FREQ_BLOCK, SEQ), lambda b, i: (i, 0)),
                  pl.BlockSpec((SEQ, 2 * FREQ_BLOCK), lambda b, i: (0, i)),
                  kspec, kspec],
        out_specs=slab,
        out_shape=jax.ShapeDtypeStruct((batch * SEQ, HYENA_WIDTH), BF16),
        scratch_shapes=[pltpu.VMEM((SEQ, HYENA_WIDTH), F32)],
        compiler_params=_params(2),
        name="hyena",
    )(x0c, uv, skip, dft_fwd, dft_inv, kr, ki)


def _attention_kernel(cls_ref, bias_ref, *refs):
    qkv_refs = refs[:3 * N_GROUPS]
    o_ref, acc_ref, den_ref, max_ref = refs[3 * N_GROUPS:]
    lane = lax.broadcasted_iota(jnp.int32, (1, LANES), 1)
    first = lane < HEAD_DIM
    ones = jnp.ones((K_WINDOW, LANES), BF16)
    n_blocks = SEQ // Q_BLOCK

    for pair in range(PAIRS_PER_STEP):
        lanes = slice(pair * LANES, (pair + 1) * LANES)
        for g, d in enumerate(DILATIONS):
            q_ref, k_ref, v_ref = qkv_refs[3 * g:3 * g + 3]
            n = SEQ // d

            def q_block(c, carry, pair=pair, lanes=lanes, g=g, d=d, n=n, q_ref=q_ref, k_ref=k_ref, v_ref=v_ref):
                q0 = pl.multiple_of(c * Q_BLOCK, Q_BLOCK)
                ws = pl.multiple_of(jnp.clip(q0 - RADIUS, 0, SEQ - K_WINDOW), RADIUS)
                q = q_ref[pl.ds(q0, Q_BLOCK), lanes]
                zero = jnp.zeros_like(q)
                q2 = jnp.concatenate([jnp.where(first, q, zero), jnp.where(first, zero, q)], axis=0)
                s = lax.dot_general(q2, k_ref[pl.ds(ws, K_WINDOW), lanes], (((1,), (1,)), ((), ())),
                                    preferred_element_type=F32)
                bias = bias_ref[cls_ref[g * n_blocks + c]]
                s = s + jnp.concatenate([bias, bias], axis=0)
                m = jnp.max(s, axis=1, keepdims=True)
                p = jnp.exp2(s - m).astype(BF16)
                out = jnp.dot(p, jnp.concatenate([v_ref[pl.ds(ws, K_WINDOW), lanes], ones], axis=1),
                              preferred_element_type=F32)
                if d == 1:
                    dst = pl.ds(q0, Q_BLOCK)
                else:
                    dst = pl.ds(q0 // n + d * lax.rem(q0, n), Q_BLOCK, stride=d)
                acc_ref[pair, g, dst, :] = jnp.where(first, out[:Q_BLOCK, :LANES], out[Q_BLOCK:, :LANES])
                den_ref[pair, g, dst, :] = jnp.where(first, out[:Q_BLOCK, LANES:], out[Q_BLOCK:, LANES:])
                max_ref[pair, g, dst, :] = jnp.where(first, m[:Q_BLOCK], m[Q_BLOCK:])
                return carry

            lax.fori_loop(0, n_blocks, q_block, 0, unroll=n_blocks)

        def combine(c, carry, pair=pair, lanes=lanes):
            rows = pl.ds(pl.multiple_of(c * Q_BLOCK, Q_BLOCK), Q_BLOCK)
            ms = [max_ref[pair, g, rows, :] for g in range(N_GROUPS)]
            top = functools.reduce(jnp.maximum, ms)
            ws = [jnp.exp2(m - top) for m in ms]
            num = sum(ws[g] * acc_ref[pair, g, rows, :] for g in range(N_GROUPS))
            den = sum(ws[g] * den_ref[pair, g, rows, :] for g in range(N_GROUPS))
            o_ref[rows, lanes] = (num / den).astype(BF16)
            return carry

        lax.fori_loop(0, n_blocks, combine, 0, unroll=2)


def _attention(qkvs, batch):
    width = PAIRS_PER_STEP * LANES
    steps = ATTN_WIDTH // width
    table, ids = (jnp.asarray(a) for a in _attention_masks())
    specs = [pl.BlockSpec(memory_space=pltpu.SMEM),
             pl.BlockSpec(table.shape, lambda b, hp: (0, 0, 0))]
    operands = []
    for g in range(N_GROUPS):
        for t in range(3):
            specs.append(pl.BlockSpec((SEQ, width), functools.partial(lambda b, hp, t: (b, t * steps + hp), t=t)))
            operands.append(qkvs[g])
    group_rows = pltpu.VMEM((PAIRS_PER_STEP, N_GROUPS, SEQ, LANES), F32)
    return pl.pallas_call(
        _attention_kernel,
        grid=(batch, steps),
        in_specs=specs,
        out_specs=pl.BlockSpec((SEQ, width), lambda b, hp: (b, hp)),
        out_shape=jax.ShapeDtypeStruct((batch * SEQ, ATTN_WIDTH), BF16),
        scratch_shapes=[group_rows, group_rows, group_rows],
        compiler_params=_params(2),
        name="attention",
    )(ids, table, *operands)


def _merge_kernel(x_ref, hn_ref, yh_ref, ya_ref, wg_ref, ph_ref, pa_ref, wo_ref, o_ref):
    hn = hn_ref[...]
    g_hy = jax.nn.sigmoid(jnp.dot(hn, wg_ref[:, :D_MODEL], preferred_element_type=F32))
    merged = g_hy * jnp.dot(yh_ref[...], ph_ref[...], preferred_element_type=F32)
    g_att = jax.nn.sigmoid(jnp.dot(hn, wg_ref[:, D_MODEL:], preferred_element_type=F32))
    merged = merged + g_att * jnp.dot(ya_ref[...], pa_ref[...], preferred_element_type=F32)
    o_ref[...] = x_ref[...] + jnp.dot(merged.astype(BF16), wo_ref[...], preferred_element_type=F32)


def _merge(x2d, hn0, y_hy, y_att, w_gate, p_hy, p_att, w_o, rows):
    tm = 1024
    row = lambda w: pl.BlockSpec((tm, w), lambda i: (i, 0))
    const = lambda a: pl.BlockSpec(a.shape, lambda i: (0, 0))
    return pl.pallas_call(
        _merge_kernel,
        grid=(rows // tm,),
        in_specs=[row(D_MODEL), row(D_MODEL), row(HYENA_WIDTH), row(ATTN_WIDTH),
                  const(w_gate), const(p_hy), const(p_att), const(w_o)],
        out_specs=row(D_MODEL),
        out_shape=jax.ShapeDtypeStruct((rows, D_MODEL), F32),
        compiler_params=_params(1),
        name="merge",
    )(x2d, hn0, y_hy, y_att, w_gate, p_hy, p_att, w_o)


def _ffn_kernel(x_ref, g_ref, w1_ref, w2_ref, gn_ref, o_ref, *next_ref, last):
    x = x_ref[...]
    hn = _rms(x, g_ref[...]).astype(BF16)
    chunk = 1024
    y = x
    for c in range(D_FF // chunk):
        h = jnp.dot(hn, w1_ref[:, c * chunk:(c + 1) * chunk], preferred_element_type=F32)
        h = jnp.square(jnp.maximum(h, 0.0)).astype(BF16)
        y = y + jnp.dot(h, w2_ref[c * chunk:(c + 1) * chunk, :], preferred_element_type=F32)
    if last:
        o_ref[...] = _rms(y, gn_ref[...])
    else:
        o_ref[...] = y
        next_ref[0][...] = _rms(y, gn_ref[...]).astype(BF16)


def _ffn(x2d, gain, w1, w2, gain_next, last):
    rows = x2d.shape[0]
    row = pl.BlockSpec((ROW_TILE, D_MODEL), lambda i: (i, 0))
    const = lambda a: pl.BlockSpec(a.shape, lambda i: (0, 0))
    gain = gain.reshape(1, D_MODEL)
    gain_next = gain_next.reshape(1, D_MODEL)
    out_shape = [jax.ShapeDtypeStruct((rows, D_MODEL), F32)]
    if not last:
        out_shape.append(jax.ShapeDtypeStruct((rows, D_MODEL), BF16))
    return pl.pallas_call(
        functools.partial(_ffn_kernel, last=last),
        grid=(rows // ROW_TILE,),
        in_specs=[row, const(gain), const(w1), const(w2), const(gain_next)],
        out_specs=[row] * len(out_shape),
        out_shape=out_shape,
        compiler_params=_params(1),
        name="ffn",
    )(x2d, gain, w1, w2, gain_next)


def kernel(x, norm_mix, w_in, conv_w, conv_b, filt_w1, filt_b1, filt_w_inner, filt_b_inner, filt_w_out,
           filt_freq, hy_skip, p_hy, p_att, w_o, norm_ffn, w_ff1, w_ff2, norm_final):
    batch, seq, d_model = x.shape
    assert (seq, d_model) == (SEQ, D_MODEL)
    depth = norm_mix.shape[0]
    rows = batch * seq
    rope = tuple(jnp.asarray(t) for t in _rope_tables())
    dft_fwd, dft_inv = (jnp.asarray(t).astype(BF16) for t in _dft_tables())
    z, window = (jnp.asarray(t) for t in _filter_tables())
    bf = lambda a: a.astype(BF16)

    x2d = x.reshape(rows, d_model)
    hn = _norm(x2d, norm_mix[0])
    for l in range(depth):
        kr, ki = _hyena_filter_spectrum(z, window, filt_w1[l], filt_b1[l], filt_w_inner[l], filt_b_inner[l],
                                        filt_w_out[l], filt_freq[l], dft_fwd)
        x0c, uv = _hy_proj(hn, bf(w_in[l, :, :HY_IN_WIDTH]), conv_w[l], conv_b[l], batch)
        y_hy = _hyena(x0c, uv, hy_skip[l], dft_fwd, dft_inv, kr, ki, batch)
        qkvs = [_qkv_proj(hn, bf(w_in[l, :, HY_IN_WIDTH + g * GROUP_WIDTH:HY_IN_WIDTH + (g + 1) * GROUP_WIDTH]),
                          rope[g], batch, d) for g, d in enumerate(DILATIONS)]
        y_att = _attention(qkvs, batch)
        x2d = _merge(x2d, hn, y_hy, y_att, bf(w_in[l, :, HY_IN_WIDTH + QKV_WIDTH:]), bf(p_hy[l]), bf(p_att[l]),
                     bf(w_o[l]), rows)
        last = l == depth - 1
        outs = _ffn(x2d, norm_ffn[l], bf(w_ff1[l]), bf(w_ff2[l]), norm_final if last else norm_mix[l + 1], last)
        x2d = outs[0]
        hn = None if last else outs[1]
    return x2d.reshape(batch, seq, d_model)
```

```python
import functools
import math

import numpy as np
import jax
import jax.numpy as jnp
from jax import lax
from jax.experimental import pallas as pl
from jax.experimental.pallas import tpu as pltpu

F32 = jnp.float32
BF16 = jnp.bfloat16

D_MODEL = 1024
SEQ = 2048
HEAD_DIM = 64
ATTN_HEADS = 8
DILATIONS = (1, 4, 16)
RADIUS = 64
N_GROUPS = len(DILATIONS)
ATTN_WIDTH = ATTN_HEADS * HEAD_DIM
ROPE_THETA = 10000.0
NEG_INF = -1e30
HYENA_WIDTH = D_MODEL // 2
FILTER_BANDS = 16
FILTER_EMB_DIM = 1 + 2 * FILTER_BANDS
FILTER_HIDDEN = 64
FILTER_INNER = 2
DECAY_TARGET = 1e-2
FAST_DECAY_PCT = 0.3
SLOW_DECAY_PCT = 1.5
HY_IN_WIDTH = 3 * HYENA_WIDTH
GROUP_WIDTH = 3 * ATTN_WIDTH
QKV_WIDTH = N_GROUPS * GROUP_WIDTH
D_FF = 4 * D_MODEL
RMS_EPS = 1e-6

LANES = 128
SUBLANES = 8
N_SLABS = D_MODEL // LANES
N_FFT = 2 * SEQ
HALF = SEQ // 2
FREQ_BLOCK = 512
N_FREQ_BLOCKS = HALF // FREQ_BLOCK
N_FILTER_SETS = 5
Q_BLOCK = 128
K_WINDOW = 256
ROW_TILE = 512
ROW_CHUNK = 512
VMEM_LIMIT = 56 * 1024 * 1024

assert DILATIONS[0] == 1 and DILATIONS[2] == DILATIONS[1] ** 2


def _params(n_axes, vmem=VMEM_LIMIT):
    return pltpu.CompilerParams(dimension_semantics=("arbitrary",) * n_axes, vmem_limit_bytes=vmem)


def _perm_positions(dilation):
    n = SEQ // dilation
    r = np.arange(SEQ) // n
    m = np.arange(SEQ) % n
    return m * dilation + r


@functools.lru_cache(maxsize=None)
def _rope_tables():
    half = HEAD_DIM // 2
    inv_freq = ROPE_THETA ** (-np.arange(half, dtype=np.float64) / half)
    lane = np.arange(LANES)
    first_half = (lane % HEAD_DIM) < half
    tables = []
    for d in DILATIONS:
        ang = _perm_positions(d)[:, None].astype(np.float64) * inv_freq[None, :]
        cos_l = np.cos(ang)[:, lane % half]
        sin_l = np.sin(ang)[:, lane % half]
        sa = np.where(first_half[None, :], -sin_l, 0.0)
        sb = np.where(first_half[None, :], 0.0, sin_l)
        tables.append(np.stack([cos_l, sa, sb]).astype(np.float32))
    return tuple(tables)


def _blocked(rows_a, rows_b, axis):
    parts = []
    for b in range(N_FREQ_BLOCKS):
        sl = [slice(None)] * rows_a.ndim
        sl[axis] = slice(b * FREQ_BLOCK, (b + 1) * FREQ_BLOCK)
        parts += [rows_a[tuple(sl)], rows_b[tuple(sl)]]
    return np.concatenate(parts, axis=axis)


@functools.lru_cache(maxsize=None)
def _fold_dft_tables():
    j = np.arange(HALF, dtype=np.int64)[:, None]
    n = np.arange(HALF, dtype=np.int64)[None, :]
    ang = lambda k: ((k * (2 * n + 1)) % (2 * N_FFT)).astype(np.float64) * (2.0 * math.pi / (2 * N_FFT))
    ce, se, co, so = np.cos(ang(2 * j)), np.sin(ang(2 * j)), np.cos(ang(2 * j + 1)), np.sin(ang(2 * j + 1))
    se[0, :] = (-1.0) ** np.arange(HALF)
    w_e = np.full((HALF, 1), 2.0 / N_FFT)
    w_e[0] = 1.0 / N_FFT
    w_o = 2.0 / N_FFT
    fwd_e = _blocked(ce, so, 0)
    fwd_o = _blocked(co, se, 0)
    inv_s = _blocked((w_e * ce).T, (-w_o * so).T, 1)
    inv_a = _blocked((-w_e * se).T, (w_o * co).T, 1)
    rev = np.eye(HALF)[::-1]
    return tuple(np.ascontiguousarray(t).astype(np.float32) for t in (fwd_e, fwd_o, inv_s, inv_a, rev))


@functools.lru_cache(maxsize=None)
def _filter_dft_table():
    j = np.arange(HALF, dtype=np.int64)[:, None]
    lag = np.arange(SEQ, dtype=np.int64)[None, :]
    ang = lambda k: ((k * lag) % N_FFT).astype(np.float64) * (2.0 * math.pi / N_FFT)
    ce, se, co, so = np.cos(ang(2 * j)), np.sin(ang(2 * j)), np.cos(ang(2 * j + 1)), np.sin(ang(2 * j + 1))
    se[0, :] = (-1.0) ** np.arange(SEQ)
    parts = []
    for b in range(N_FREQ_BLOCKS):
        rows = slice(b * FREQ_BLOCK, (b + 1) * FREQ_BLOCK)
        parts += [ce[rows], se[rows], co[rows], so[rows]]
    return np.concatenate(parts, axis=0).astype(np.float32)


@functools.lru_cache(maxsize=None)
def _filter_tables():
    n = np.arange(SEQ, dtype=np.float64)
    t = n / max(SEQ - 1, 1)
    bands = np.linspace(1e-4, FILTER_BANDS - 1, FILTER_BANDS)
    ang = (2.0 * math.pi / SEQ) * n[:, None] * bands[None, :]
    z = np.concatenate([t[:, None], np.cos(ang), -np.sin(ang)], axis=-1)
    z = np.pad(z, ((0, 0), (0, LANES - FILTER_EMB_DIM)))
    max_decay = math.log(DECAY_TARGET) / FAST_DECAY_PCT
    min_decay = math.log(DECAY_TARGET) / SLOW_DECAY_PCT
    deltas = np.abs(np.linspace(min_decay, max_decay, HYENA_WIDTH))
    window = np.exp(-t[:, None] * deltas[None, :])
    return z.astype(np.float32), window.astype(np.float32)


@functools.lru_cache(maxsize=None)
def _attention_masks():
    i = np.arange(Q_BLOCK)[:, None]
    j = np.arange(K_WINDOW)[None, :]
    classes, ids = {}, []
    for d in DILATIONS:
        n = SEQ // d
        for c in range(SEQ // Q_BLOCK):
            q0 = c * Q_BLOCK
            ws = min(max(q0 - RADIUS, 0), SEQ - K_WINDOW)
            seg = (q0 // n) * n
            kpos = ws + j
            valid = (np.abs(kpos - (q0 + i)) <= RADIUS) & (kpos >= seg) & (kpos < seg + n)
            bias = np.where(valid, 0.0, NEG_INF).astype(np.float32)
            ids.append(classes.setdefault(bias.tobytes(), len(classes)))
    table = np.stack([np.frombuffer(b, np.float32).reshape(Q_BLOCK, K_WINDOW) for b in classes])
    return table, np.asarray(ids, np.int32)


def _rms(x, gain):
    return x * lax.rsqrt(jnp.mean(x * x, axis=-1, keepdims=True) + RMS_EPS) * gain


def _store_row_orders(hn, o0_ref, o1_ref, o2_ref, a_ref, b_ref):
    d1, d2 = DILATIONS[1], DILATIONS[2]
    n1, n2 = ROW_TILE // d1, ROW_TILE // d2
    o0_ref[...] = hn.astype(BF16)
    for s in range(N_SLABS):
        a_ref[s] = hn[:, s * LANES:(s + 1) * LANES]
    for r in range(d1):
        v = jnp.concatenate([a_ref[s, pl.ds(r, n1, stride=d1), :] for s in range(N_SLABS)], axis=1)
        o1_ref[r] = v.astype(BF16)
        for s in range(N_SLABS):
            b_ref[s, r * n1:(r + 1) * n1, :] = v[:, s * LANES:(s + 1) * LANES]
    for r in range(d2):
        src = pl.ds((r % d1) * n1 + r // d1, n2, stride=d1)
        v = jnp.concatenate([b_ref[s, src, :] for s in range(N_SLABS)], axis=1)
        o2_ref[r] = v.astype(BF16)


def _row_order_outputs(batch):
    d1, d2 = DILATIONS[1], DILATIONS[2]
    tiles = SEQ // ROW_TILE
    shapes = [jax.ShapeDtypeStruct((batch * SEQ, D_MODEL), BF16),
              jax.ShapeDtypeStruct((batch, d1, SEQ // d1, D_MODEL), BF16),
              jax.ShapeDtypeStruct((batch, d2, SEQ // d2, D_MODEL), BF16)]
    specs = [pl.BlockSpec((ROW_TILE, D_MODEL), lambda i: (i, 0)),
             pl.BlockSpec((None, d1, ROW_TILE // d1, D_MODEL), lambda i: (i // tiles, 0, i % tiles, 0)),
             pl.BlockSpec((None, d2, ROW_TILE // d2, D_MODEL), lambda i: (i // tiles, 0, i % tiles, 0))]
    scratch = [pltpu.VMEM((N_SLABS, ROW_TILE, LANES), F32), pltpu.VMEM((N_SLABS, ROW_TILE, LANES), F32)]
    return shapes, specs, scratch


def _norm_perm_kernel(x_ref, g_ref, o0_ref, o1_ref, o2_ref, a_ref, b_ref):
    _store_row_orders(_rms(x_ref[...], g_ref[...]), o0_ref, o1_ref, o2_ref, a_ref, b_ref)


def _norm_perm(x2d, gain, batch):
    shapes, specs, scratch = _row_order_outputs(batch)
    return pl.pallas_call(
        _norm_perm_kernel,
        grid=(batch * SEQ // ROW_TILE,),
        in_specs=[pl.BlockSpec((ROW_TILE, D_MODEL), lambda i: (i, 0)), pl.BlockSpec((1, D_MODEL), lambda i: (0, 0))],
        out_specs=specs,
        out_shape=shapes,
        scratch_shapes=scratch,
        compiler_params=_params(1),
        name="norm_perm",
    )(x2d, gain.reshape(1, D_MODEL))


def _filter_kernel(z_ref, win_ref, w1_ref, b1_ref, wi_ref, bi_ref, wo_ref, fr_ref, kf_ref, o_ref, hcat_ref):
    i = pl.program_id(0)
    rows_per_chunk = 256
    hi = lax.Precision.HIGHEST
    width = HYENA_WIDTH

    @pl.when(i == 0)
    def _():
        fr = fr_ref[...]

        def chunk(c, carry):
            r0 = pl.multiple_of(c * rows_per_chunk, rows_per_chunk)
            rows = pl.ds(r0, rows_per_chunk)
            hid = jnp.sin(fr * (jnp.dot(z_ref[rows, :], w1_ref[...], precision=hi,
                                        preferred_element_type=F32) + b1_ref[...]))
            for l in range(FILTER_INNER):
                hid = jnp.sin(fr * (jnp.dot(hid, wi_ref[l], precision=hi,
                                            preferred_element_type=F32) + bi_ref[l]))
            filt = jnp.dot(hid, wo_ref[...], precision=hi, preferred_element_type=F32)
            win = win_ref[rows, :]
            row = r0 + lax.broadcasted_iota(jnp.int32, (rows_per_chunk, 1), 0)
            h_fwd = filt[:, :width] * win
            h_bwd = jnp.where(row == 0, 0.0, filt[:, width:] * win)
            hcat_ref[rows, :width] = h_fwd.astype(BF16)
            hcat_ref[rows, width:] = h_bwd.astype(BF16)
            return carry

        lax.fori_loop(0, SEQ // rows_per_chunk, chunk, 0)

    def sums(g):
        sp = jnp.dot(kf_ref[g * FREQ_BLOCK:(g + 1) * FREQ_BLOCK, :], hcat_ref[...], preferred_element_type=F32)
        return sp[:, :width], sp[:, width:]

    first_row = (lax.broadcasted_iota(jnp.int32, (FREQ_BLOCK, 1), 0) == 0) & (i == 0)
    f, b = sums(0)
    re_even = f + b
    f, b = sums(1)
    nyquist = f[0:1] + b[0:1]
    o_ref[0] = re_even
    o_ref[1] = jnp.where(first_row, 0.0, b - f)
    o_ref[2] = jnp.where(first_row, nyquist, re_even)
    f, b = sums(2)
    o_ref[3] = f + b
    f, b = sums(3)
    o_ref[4] = b - f


def _hyena_filter_spectrum(z, window, w1, b1, w_inner, b_inner, w_out, freq, filter_dft):
    full = lambda a: pl.BlockSpec(a.shape, lambda i: (0,) * a.ndim)
    w1p = jnp.pad(w1, ((0, LANES - FILTER_EMB_DIM), (0, 0)))
    b1 = b1.reshape(1, FILTER_HIDDEN)
    b_inner = b_inner.reshape(FILTER_INNER, 1, FILTER_HIDDEN)
    freq = freq.reshape(1, FILTER_HIDDEN)
    ops = (z, window, w1p, b1, w_inner, b_inner, w_out, freq)
    return pl.pallas_call(
        _filter_kernel,
        grid=(N_FREQ_BLOCKS,),
        in_specs=[full(a) for a in ops] + [pl.BlockSpec((4 * FREQ_BLOCK, SEQ), lambda i: (i, 0))],
        out_specs=pl.BlockSpec((None, N_FILTER_SETS, FREQ_BLOCK, HYENA_WIDTH), lambda i: (i, 0, 0, 0)),
        out_shape=jax.ShapeDtypeStruct((N_FREQ_BLOCKS, N_FILTER_SETS, FREQ_BLOCK, HYENA_WIDTH), F32),
        scratch_shapes=[pltpu.VMEM((SEQ, 2 * HYENA_WIDTH), BF16)],
        compiler_params=_params(1),
        name="hyena_filter",
    )(*ops, filter_dft)


def _hy_proj_kernel(hn_ref, w_ref, cw_ref, cb_ref, x0_ref, uv_ref):
    n_chunks = SEQ // ROW_CHUNK
    pre = [jnp.dot(hn_ref[rc * ROW_CHUNK:(rc + 1) * ROW_CHUNK, :], w_ref[...], preferred_element_type=F32)
           for rc in range(n_chunks)]
    w = cw_ref[...]
    bias = cb_ref[...]
    zeros = jnp.zeros((SUBLANES, HY_IN_WIDTH), F32)
    for rc in range(n_chunks):
        before = pre[rc - 1][ROW_CHUNK - SUBLANES:] if rc > 0 else zeros
        after = pre[rc + 1][:SUBLANES] if rc < n_chunks - 1 else zeros
        ext = ROW_CHUNK + SUBLANES
        prev = pltpu.roll(jnp.concatenate([before, pre[rc]], axis=0), 1, 0)[SUBLANES:]
        nxt = pltpu.roll(jnp.concatenate([pre[rc], after], axis=0), ext - 1, 0)[:ROW_CHUNK]
        u = bias + prev * w[0:1] + pre[rc] * w[1:2] + nxt * w[2:3]
        rows = slice(rc * ROW_CHUNK, (rc + 1) * ROW_CHUNK)
        x0_ref[rows, :] = u[:, :HYENA_WIDTH].astype(BF16)
        uv_ref[rows, :] = (u[:, HYENA_WIDTH:2 * HYENA_WIDTH] * u[:, 2 * HYENA_WIDTH:]).astype(BF16)


def _hy_proj(hn0, w_hy, conv_w, conv_b, batch):
    const = lambda a: pl.BlockSpec(a.shape, lambda b: (0,) * a.ndim)
    conv_b = conv_b.reshape(1, HY_IN_WIDTH)
    out_spec = pl.BlockSpec((SEQ, HYENA_WIDTH), lambda b: (b, 0))
    out_shape = jax.ShapeDtypeStruct((batch * SEQ, HYENA_WIDTH), BF16)
    return pl.pallas_call(
        _hy_proj_kernel,
        grid=(batch,),
        in_specs=[pl.BlockSpec((SEQ, D_MODEL), lambda b: (b, 0)), const(w_hy), const(conv_w), const(conv_b)],
        out_specs=[out_spec, out_spec],
        out_shape=[out_shape, out_shape],
        compiler_params=_params(1),
        name="hy_proj",
    )(hn0, w_hy, conv_w, conv_b)


def _qkv_proj_kernel(hn_ref, w_ref, rope_ref, o_ref):
    half = HEAD_DIM // 2
    q_scale = HEAD_DIM ** -0.5 * math.log2(math.e)
    for rc in range(SEQ // ROW_CHUNK):
        rows = slice(rc * ROW_CHUNK, (rc + 1) * ROW_CHUNK)
        acc = jnp.dot(hn_ref[rows, :], w_ref[...], preferred_element_type=F32)
        cos, sa, sb = rope_ref[0, rows, :], rope_ref[1, rows, :], rope_ref[2, rows, :]
        for c in range(2 * ATTN_WIDTH // LANES):
            t = acc[:, c * LANES:(c + 1) * LANES]
            if c < ATTN_WIDTH // LANES:
                t = t * q_scale
            out = t * cos + pltpu.roll(t, LANES - half, 1) * sa + pltpu.roll(t, half, 1) * sb
            o_ref[rows, c * LANES:(c + 1) * LANES] = out.astype(BF16)
        o_ref[rows, 2 * ATTN_WIDTH:] = acc[:, 2 * ATTN_WIDTH:].astype(BF16)


def _qkv_proj(hn, w_qkv, rope, batch):
    const = lambda a: pl.BlockSpec(a.shape, lambda b: (0,) * a.ndim)
    return pl.pallas_call(
        _qkv_proj_kernel,
        grid=(batch,),
        in_specs=[pl.BlockSpec((SEQ, D_MODEL), lambda b: (b, 0)), const(w_qkv), const(rope)],
        out_specs=pl.BlockSpec((SEQ, GROUP_WIDTH), lambda b: (b, 0)),
        out_shape=jax.ShapeDtypeStruct((batch * SEQ, GROUP_WIDTH), BF16),
        compiler_params=_params(1),
        name="qkv_proj",
    )(hn, w_qkv, rope)


def _hyena_kernel(x0_ref, uv_ref, skip_ref, rev_ref, fe_ref, fo_ref, gs_ref, ga_ref, h_ref,
                  o_ref, e_ref, d_ref, s_ref, a_ref):
    i = pl.program_id(1)

    @pl.when(i == 0)
    def _():
        lo = uv_ref[:HALF, :].astype(F32)
        mirrored = jnp.dot(rev_ref[...], uv_ref[HALF:, :], preferred_element_type=F32)
        e_ref[...] = (lo + mirrored).astype(BF16)
        d_ref[...] = (lo - mirrored).astype(BF16)
        s_ref[...] = jnp.zeros_like(s_ref)
        a_ref[...] = jnp.zeros_like(a_ref)

    from_e = jnp.dot(fe_ref[...], e_ref[...], preferred_element_type=F32)
    from_d = jnp.dot(fo_ref[...], d_ref[...], preferred_element_type=F32)
    a_e, b_o = from_e[:FREQ_BLOCK], from_e[FREQ_BLOCK:]
    a_o, b_e = from_d[:FREQ_BLOCK], from_d[FREQ_BLOCK:]
    yr_e = a_e * h_ref[0] + b_e * h_ref[1]
    yi_e = a_e * h_ref[1] - b_e * h_ref[2]
    yr_o = a_o * h_ref[3] + b_o * h_ref[4]
    yi_o = a_o * h_ref[4] - b_o * h_ref[3]
    s_ref[...] += jnp.dot(gs_ref[...], jnp.concatenate([yr_e, yi_o], axis=0).astype(BF16),
                          preferred_element_type=F32)
    a_ref[...] += jnp.dot(ga_ref[...], jnp.concatenate([yi_e, yr_o], axis=0).astype(BF16),
                          preferred_element_type=F32)

    @pl.when(i == N_FREQ_BLOCKS - 1)
    def _():
        sym, anti = s_ref[...], a_ref[...]
        skip = skip_ref[...]
        y_lo = sym + anti + skip * uv_ref[:HALF, :].astype(F32)
        y_hi = (jnp.dot(rev_ref[...], (sym - anti).astype(BF16), preferred_element_type=F32)
                + skip * uv_ref[HALF:, :].astype(F32))
        o_ref[:HALF, :] = (x0_ref[:HALF, :].astype(F32) * y_lo).astype(BF16)
        o_ref[HALF:, :] = (x0_ref[HALF:, :].astype(F32) * y_hi).astype(BF16)


def _hyena(x0c, uv, skip, fold_dft, filters, batch):
    fwd_e, fwd_o, inv_s, inv_a, rev = fold_dft
    slab = pl.BlockSpec((SEQ, HYENA_WIDTH), lambda b, i: (b, 0))
    fwd_spec = pl.BlockSpec((2 * FREQ_BLOCK, HALF), lambda b, i: (i, 0))
    inv_spec = pl.BlockSpec((HALF, 2 * FREQ_BLOCK), lambda b, i: (0, i))
    half_rows = pltpu.VMEM((HALF, HYENA_WIDTH), BF16)
    half_acc = pltpu.VMEM((HALF, HYENA_WIDTH), F32)
    return pl.pallas_call(
        _hyena_kernel,
        grid=(batch, N_FREQ_BLOCKS),
        in_specs=[slab, slab, pl.BlockSpec((1, HYENA_WIDTH), lambda b, i: (0, 0)),
                  pl.BlockSpec((HALF, HALF), lambda b, i: (0, 0)),
                  fwd_spec, fwd_spec, inv_spec, inv_spec,
                  pl.BlockSpec((None, N_FILTER_SETS, FREQ_BLOCK, HYENA_WIDTH), lambda b, i: (i, 0, 0, 0))],
        out_specs=slab,
        out_shape=jax.ShapeDtypeStruct((batch * SEQ, HYENA_WIDTH), BF16),
        scratch_shapes=[half_rows, half_rows, half_acc, half_acc],
        compiler_params=_params(2),
        name="hyena",
    )(x0c, uv, skip.reshape(1, HYENA_WIDTH), rev, fwd_e, fwd_o, inv_s, inv_a, filters)


def _attention_kernel(cls_ref, bias_ref, *refs):
    qkv_refs = refs[:3 * N_GROUPS]
    o_ref, acc_ref, den_ref, max_ref = refs[3 * N_GROUPS:]
    lane = lax.broadcasted_iota(jnp.int32, (1, LANES), 1)
    first = lane < HEAD_DIM
    ones = jnp.ones((K_WINDOW, LANES), BF16)
    n_blocks = SEQ // Q_BLOCK

    for g, d in enumerate(DILATIONS):
        q_ref, k_ref, v_ref = qkv_refs[3 * g:3 * g + 3]
        n = SEQ // d

        def q_block(c, carry, g=g, d=d, n=n, q_ref=q_ref, k_ref=k_ref, v_ref=v_ref):
            q0 = pl.multiple_of(c * Q_BLOCK, Q_BLOCK)
            ws = pl.multiple_of(jnp.clip(q0 - RADIUS, 0, SEQ - K_WINDOW), RADIUS)
            q = q_ref[pl.ds(q0, Q_BLOCK), :]
            zero = jnp.zeros_like(q)
            q2 = jnp.concatenate([jnp.where(first, q, zero), jnp.where(first, zero, q)], axis=0)
            s = lax.dot_general(q2, k_ref[pl.ds(ws, K_WINDOW), :], (((1,), (1,)), ((), ())),
                                preferred_element_type=F32)
            bias = bias_ref[cls_ref[g * n_blocks + c]]
            s = s + jnp.concatenate([bias, bias], axis=0)
            m = jnp.max(s, axis=1, keepdims=True)
            p = jnp.exp2(s - m).astype(BF16)
            out = jnp.dot(p, jnp.concatenate([v_ref[pl.ds(ws, K_WINDOW), :], ones], axis=1),
                          preferred_element_type=F32)
            if d == 1:
                dst = pl.ds(q0, Q_BLOCK)
            else:
                dst = pl.ds(q0 // n + d * lax.rem(q0, n), Q_BLOCK, stride=d)
            acc_ref[g, dst, :] = jnp.where(first, out[:Q_BLOCK, :LANES], out[Q_BLOCK:, :LANES])
            den_ref[g, dst, :] = jnp.where(first, out[:Q_BLOCK, LANES:], out[Q_BLOCK:, LANES:])
            max_ref[g, dst, :] = jnp.where(first, m[:Q_BLOCK], m[Q_BLOCK:])
            return carry

        lax.fori_loop(0, n_blocks, q_block, 0, unroll=n_blocks)

    def combine(c, carry):
        rows = pl.ds(pl.multiple_of(c * Q_BLOCK, Q_BLOCK), Q_BLOCK)
        ms = [max_ref[g, rows, :] for g in range(N_GROUPS)]
        top = functools.reduce(jnp.maximum, ms)
        ws = [jnp.exp2(m - top) for m in ms]
        num = sum(ws[g] * acc_ref[g, rows, :] for g in range(N_GROUPS))
        den = sum(ws[g] * den_ref[g, rows, :] for g in range(N_GROUPS))
        o_ref[rows, :] = (num / den).astype(BF16)
        return carry

    lax.fori_loop(0, n_blocks, combine, 0, unroll=2)


def _attention(qkvs, batch):
    pairs = ATTN_WIDTH // LANES
    table, ids = (jnp.asarray(a) for a in _attention_masks())
    specs = [pl.BlockSpec(memory_space=pltpu.SMEM),
             pl.BlockSpec(table.shape, lambda b, hp: (0, 0, 0))]
    operands = []
    for g in range(N_GROUPS):
        for t in range(3):
            specs.append(pl.BlockSpec((SEQ, LANES), functools.partial(lambda b, hp, t: (b, t * pairs + hp), t=t)))
            operands.append(qkvs[g])
    group_rows = pltpu.VMEM((N_GROUPS, SEQ, LANES), F32)
    return pl.pallas_call(
        _attention_kernel,
        grid=(batch, pairs),
        in_specs=specs,
        out_specs=pl.BlockSpec((SEQ, LANES), lambda b, hp: (b, hp)),
        out_shape=jax.ShapeDtypeStruct((batch * SEQ, ATTN_WIDTH), BF16),
        scratch_shapes=[group_rows, group_rows, group_rows],
        compiler_params=_params(2),
        name="attention",
    )(ids, table, *operands)


def _merge_kernel(x_ref, hn_ref, yh_ref, ya_ref, wg_ref, ph_ref, pa_ref, wo_ref, o_ref):
    hn = hn_ref[...]
    g_hy = jax.nn.sigmoid(jnp.dot(hn, wg_ref[:, :D_MODEL], preferred_element_type=F32))
    merged = g_hy * jnp.dot(yh_ref[...], ph_ref[...], preferred_element_type=F32)
    g_att = jax.nn.sigmoid(jnp.dot(hn, wg_ref[:, D_MODEL:], preferred_element_type=F32))
    merged = merged + g_att * jnp.dot(ya_ref[...], pa_ref[...], preferred_element_type=F32)
    o_ref[...] = x_ref[...] + jnp.dot(merged.astype(BF16), wo_ref[...], preferred_element_type=F32)


def _merge(x2d, hn0, y_hy, y_att, w_gate, p_hy, p_att, w_o, rows):
    tm = 1024
    row = lambda w: pl.BlockSpec((tm, w), lambda i: (i, 0))
    const = lambda a: pl.BlockSpec(a.shape, lambda i: (0, 0))
    return pl.pallas_call(
        _merge_kernel,
        grid=(rows // tm,),
        in_specs=[row(D_MODEL), row(D_MODEL), row(HYENA_WIDTH), row(ATTN_WIDTH),
                  const(w_gate), const(p_hy), const(p_att), const(w_o)],
        out_specs=row(D_MODEL),
        out_shape=jax.ShapeDtypeStruct((rows, D_MODEL), F32),
        compiler_params=_params(1),
        name="merge",
    )(x2d, hn0, y_hy, y_att, w_gate, p_hy, p_att, w_o)


def _ffn_kernel(x_ref, g_ref, w1_ref, w2_ref, gn_ref, *rest, last):
    x = x_ref[...]
    hn = _rms(x, g_ref[...]).astype(BF16)
    chunk = 1024
    y = x
    for c in range(D_FF // chunk):
        h = jnp.dot(hn, w1_ref[:, c * chunk:(c + 1) * chunk], preferred_element_type=F32)
        h = jnp.square(jnp.maximum(h, 0.0)).astype(BF16)
        y = y + jnp.dot(h, w2_ref[c * chunk:(c + 1) * chunk, :], preferred_element_type=F32)
    if last:
        (o_ref,) = rest
        o_ref[...] = _rms(y, gn_ref[...])
    else:
        o_ref, o0_ref, o1_ref, o2_ref, a_ref, b_ref = rest
        o_ref[...] = y
        _store_row_orders(_rms(y, gn_ref[...]), o0_ref, o1_ref, o2_ref, a_ref, b_ref)


def _ffn(x2d, gain, w1, w2, gain_next, batch, last):
    rows = batch * SEQ
    row = pl.BlockSpec((ROW_TILE, D_MODEL), lambda i: (i, 0))
    const = lambda a: pl.BlockSpec(a.shape, lambda i: (0, 0))
    gain = gain.reshape(1, D_MODEL)
    gain_next = gain_next.reshape(1, D_MODEL)
    out_shape = [jax.ShapeDtypeStruct((rows, D_MODEL), F32)]
    out_specs = [row]
    scratch = []
    if not last:
        shapes, specs, scratch = _row_order_outputs(batch)
        out_shape += shapes
        out_specs += specs
    return pl.pallas_call(
        functools.partial(_ffn_kernel, last=last),
        grid=(rows // ROW_TILE,),
        in_specs=[row, const(gain), const(w1), const(w2), const(gain_next)],
        out_specs=out_specs,
        out_shape=out_shape,
        scratch_shapes=scratch,
        compiler_params=_params(1),
        name="ffn",
    )(x2d, gain, w1, w2, gain_next)


def kernel(x, norm_mix, w_in, conv_w, conv_b, filt_w1, filt_b1, filt_w_inner, filt_b_inner, filt_w_out,
           filt_freq, hy_skip, p_hy, p_att, w_o, norm_ffn, w_ff1, w_ff2, norm_final):
    batch, seq, d_model = x.shape
    assert (seq, d_model) == (SEQ, D_MODEL)
    depth = norm_mix.shape[0]
    rows = batch * seq
    bf = lambda a: a.astype(BF16)
    rope = tuple(jnp.asarray(t) for t in _rope_tables())
    fold_dft = tuple(bf(jnp.asarray(t)) for t in _fold_dft_tables())
    filter_dft = bf(jnp.asarray(_filter_dft_table()))
    z, window = (jnp.asarray(t) for t in _filter_tables())

    x2d = x.reshape(rows, d_model)
    hns = _norm_perm(x2d, norm_mix[0], batch)
    for l in range(depth):
        hns = [h.reshape(rows, d_model) for h in hns]
        filters = _hyena_filter_spectrum(z, window, filt_w1[l], filt_b1[l], filt_w_inner[l], filt_b_inner[l],
                                         filt_w_out[l], filt_freq[l], filter_dft)
        x0c, uv = _hy_proj(hns[0], bf(w_in[l, :, :HY_IN_WIDTH]), conv_w[l], conv_b[l], batch)
        y_hy = _hyena(x0c, uv, hy_skip[l], fold_dft, filters, batch)
        qkvs = [_qkv_proj(hns[g], bf(w_in[l, :, HY_IN_WIDTH + g * GROUP_WIDTH:HY_IN_WIDTH + (g + 1) * GROUP_WIDTH]),
                          rope[g], batch) for g in range(N_GROUPS)]
        y_att = _attention(qkvs, batch)
        x2d = _merge(x2d, hns[0], y_hy, y_att, bf(w_in[l, :, HY_IN_WIDTH + QKV_WIDTH:]), bf(p_hy[l]), bf(p_att[l]),
                     bf(w_o[l]), rows)
        last = l == depth - 1
        outs = _ffn(x2d, norm_ffn[l], bf(w_ff1[l]), bf(w_ff2[l]), norm_final if last else norm_mix[l + 1], batch, last)
        x2d, hns = outs[0], outs[1:]
    return x2d.reshape(batch, seq, d_model)
```

```python
import functools
import math

import numpy as np
import jax
import jax.numpy as jnp
from jax import lax
from jax.experimental import pallas as pl
from jax.experimental.pallas import tpu as pltpu

F32 = jnp.float32
BF16 = jnp.bfloat16

D_MODEL = 1024
SEQ = 2048
HEAD_DIM = 64
ATTN_HEADS = 8
DILATIONS = (1, 4, 16)
RADIUS = 64
N_GROUPS = len(DILATIONS)
ATTN_WIDTH = ATTN_HEADS * HEAD_DIM
ROPE_THETA = 10000.0
NEG_INF = -1e30
HYENA_WIDTH = D_MODEL // 2
FILTER_BANDS = 16
FILTER_EMB_DIM = 1 + 2 * FILTER_BANDS
FILTER_HIDDEN = 64
FILTER_INNER = 2
DECAY_TARGET = 1e-2
FAST_DECAY_PCT = 0.3
SLOW_DECAY_PCT = 1.5
HY_IN_WIDTH = 3 * HYENA_WIDTH
GROUP_WIDTH = 3 * ATTN_WIDTH
QKV_WIDTH = N_GROUPS * GROUP_WIDTH
D_FF = 4 * D_MODEL
RMS_EPS = 1e-6

LANES = 128
SUBLANES = 8
N_SLABS = D_MODEL // LANES
N_FFT = 2 * SEQ
HALF = SEQ // 2
FREQ_BLOCK = 512
N_FREQ_BLOCKS = HALF // FREQ_BLOCK
N_FILTER_SETS = 5
Q_BLOCK = 128
K_WINDOW = 256
ROW_TILE = 512
ROW_CHUNK = 512
VMEM_LIMIT = 56 * 1024 * 1024

assert DILATIONS[0] == 1 and DILATIONS[2] == DILATIONS[1] ** 2


def _params(n_axes, vmem=VMEM_LIMIT):
    return pltpu.CompilerParams(dimension_semantics=("arbitrary",) * n_axes, vmem_limit_bytes=vmem)


def _perm_positions(dilation):
    n = SEQ // dilation
    r = np.arange(SEQ) // n
    m = np.arange(SEQ) % n
    return m * dilation + r


@functools.lru_cache(maxsize=None)
def _rope_tables():
    half = HEAD_DIM // 2
    inv_freq = ROPE_THETA ** (-np.arange(half, dtype=np.float64) / half)
    lane = np.arange(LANES)
    first_half = (lane % HEAD_DIM) < half
    tables = []
    for d in DILATIONS:
        ang = _perm_positions(d)[:, None].astype(np.float64) * inv_freq[None, :]
        cos_l = np.cos(ang)[:, lane % half]
        sin_l = np.sin(ang)[:, lane % half]
        sa = np.where(first_half[None, :], -sin_l, 0.0)
        sb = np.where(first_half[None, :], 0.0, sin_l)
        tables.append(np.stack([cos_l, sa, sb]).astype(np.float32))
    return tuple(tables)


def _blocked(rows_a, rows_b, axis):
    parts = []
    for b in range(N_FREQ_BLOCKS):
        sl = [slice(None)] * rows_a.ndim
        sl[axis] = slice(b * FREQ_BLOCK, (b + 1) * FREQ_BLOCK)
        parts += [rows_a[tuple(sl)], rows_b[tuple(sl)]]
    return np.concatenate(parts, axis=axis)


@functools.lru_cache(maxsize=None)
def _fold_dft_tables():
    j = np.arange(HALF, dtype=np.int64)[:, None]
    n = np.arange(HALF, dtype=np.int64)[None, :]
    ang = lambda k: ((k * (2 * n + 1)) % (2 * N_FFT)).astype(np.float64) * (2.0 * math.pi / (2 * N_FFT))
    ce, se, co, so = np.cos(ang(2 * j)), np.sin(ang(2 * j)), np.cos(ang(2 * j + 1)), np.sin(ang(2 * j + 1))
    se[0, :] = (-1.0) ** np.arange(HALF)
    w_e = np.full((HALF, 1), 2.0 / N_FFT)
    w_e[0] = 1.0 / N_FFT
    w_o = 2.0 / N_FFT
    fwd_e = _blocked(ce, so, 0)
    fwd_o = _blocked(co, se, 0)
    inv_s = _blocked((w_e * ce).T, (-w_o * so).T, 1)
    inv_a = _blocked((-w_e * se).T, (w_o * co).T, 1)
    rev = np.eye(HALF)[::-1]
    return tuple(np.ascontiguousarray(t).astype(np.float32) for t in (fwd_e, fwd_o, inv_s, inv_a, rev))


@functools.lru_cache(maxsize=None)
def _filter_dft_table():
    j = np.arange(HALF, dtype=np.int64)[:, None]
    lag = np.arange(SEQ, dtype=np.int64)[None, :]
    ang = lambda k: ((k * lag) % N_FFT).astype(np.float64) * (2.0 * math.pi / N_FFT)
    ce, se, co, so = np.cos(ang(2 * j)), np.sin(ang(2 * j)), np.cos(ang(2 * j + 1)), np.sin(ang(2 * j + 1))
    se[0, :] = (-1.0) ** np.arange(SEQ)
    parts = []
    for b in range(N_FREQ_BLOCKS):
        rows = slice(b * FREQ_BLOCK, (b + 1) * FREQ_BLOCK)
        parts += [ce[rows], se[rows], co[rows], so[rows]]
    return np.concatenate(parts, axis=0).astype(np.float32)


@functools.lru_cache(maxsize=None)
def _filter_tables():
    n = np.arange(SEQ, dtype=np.float64)
    t = n / max(SEQ - 1, 1)
    bands = np.linspace(1e-4, FILTER_BANDS - 1, FILTER_BANDS)
    ang = (2.0 * math.pi / SEQ) * n[:, None] * bands[None, :]
    z = np.concatenate([t[:, None], np.cos(ang), -np.sin(ang)], axis=-1)
    z = np.pad(z, ((0, 0), (0, LANES - FILTER_EMB_DIM)))
    max_decay = math.log(DECAY_TARGET) / FAST_DECAY_PCT
    min_decay = math.log(DECAY_TARGET) / SLOW_DECAY_PCT
    deltas = np.abs(np.linspace(min_decay, max_decay, HYENA_WIDTH))
    window = np.exp(-t[:, None] * deltas[None, :])
    return z.astype(np.float32), window.astype(np.float32)


@functools.lru_cache(maxsize=None)
def _attention_masks():
    i = np.arange(Q_BLOCK)[:, None]
    j = np.arange(K_WINDOW)[None, :]
    classes, ids = {}, []
    for d in DILATIONS:
        n = SEQ // d
        for c in range(SEQ // Q_BLOCK):
            q0 = c * Q_BLOCK
            ws = min(max(q0 - RADIUS, 0), SEQ - K_WINDOW)
            seg = (q0 // n) * n
            kpos = ws + j
            valid = (np.abs(kpos - (q0 + i)) <= RADIUS) & (kpos >= seg) & (kpos < seg + n)
            bias = np.where(valid, 0.0, NEG_INF).astype(np.float32)
            ids.append(classes.setdefault(bias.tobytes(), len(classes)))
    table = np.stack([np.frombuffer(b, np.float32).reshape(Q_BLOCK, K_WINDOW) for b in classes])
    return table, np.asarray(ids, np.int32)


def _rms(x, gain):
    return x * lax.rsqrt(jnp.mean(x * x, axis=-1, keepdims=True) + RMS_EPS) * gain


def _store_row_orders(hn, o0_ref, o1_ref, o2_ref, a_ref, b_ref):
    d1, d2 = DILATIONS[1], DILATIONS[2]
    n1, n2 = ROW_TILE // d1, ROW_TILE // d2
    o0_ref[...] = hn.astype(BF16)
    for s in range(N_SLABS):
        a_ref[s] = hn[:, s * LANES:(s + 1) * LANES]
    for r in range(d1):
        v = jnp.concatenate([a_ref[s, pl.ds(r, n1, stride=d1), :] for s in range(N_SLABS)], axis=1)
        o1_ref[r] = v.astype(BF16)
        for s in range(N_SLABS):
            b_ref[s, r * n1:(r + 1) * n1, :] = v[:, s * LANES:(s + 1) * LANES]
    for r in range(d2):
        src = pl.ds((r % d1) * n1 + r // d1, n2, stride=d1)
        v = jnp.concatenate([b_ref[s, src, :] for s in range(N_SLABS)], axis=1)
        o2_ref[r] = v.astype(BF16)


def _row_order_outputs(batch):
    d1, d2 = DILATIONS[1], DILATIONS[2]
    tiles = SEQ // ROW_TILE
    shapes = [jax.ShapeDtypeStruct((batch * SEQ, D_MODEL), BF16),
              jax.ShapeDtypeStruct((batch, d1, SEQ // d1, D_MODEL), BF16),
              jax.ShapeDtypeStruct((batch, d2, SEQ // d2, D_MODEL), BF16)]
    specs = [pl.BlockSpec((ROW_TILE, D_MODEL), lambda i: (i, 0)),
             pl.BlockSpec((None, d1, ROW_TILE // d1, D_MODEL), lambda i: (i // tiles, 0, i % tiles, 0)),
             pl.BlockSpec((None, d2, ROW_TILE // d2, D_MODEL), lambda i: (i // tiles, 0, i % tiles, 0))]
    scratch = [pltpu.VMEM((N_SLABS, ROW_TILE, LANES), F32), pltpu.VMEM((N_SLABS, ROW_TILE, LANES), F32)]
    return shapes, specs, scratch


def _norm_perm_kernel(x_ref, g_ref, o0_ref, o1_ref, o2_ref, a_ref, b_ref):
    _store_row_orders(_rms(x_ref[...], g_ref[...]), o0_ref, o1_ref, o2_ref, a_ref, b_ref)


def _norm_perm(x2d, gain, batch):
    shapes, specs, scratch = _row_order_outputs(batch)
    return pl.pallas_call(
        _norm_perm_kernel,
        grid=(batch * SEQ // ROW_TILE,),
        in_specs=[pl.BlockSpec((ROW_TILE, D_MODEL), lambda i: (i, 0)), pl.BlockSpec((1, D_MODEL), lambda i: (0, 0))],
        out_specs=specs,
        out_shape=shapes,
        scratch_shapes=scratch,
        compiler_params=_params(1),
        name="norm_perm",
    )(x2d, gain.reshape(1, D_MODEL))


def _filter_kernel(z_ref, win_ref, w1_ref, b1_ref, wi_ref, bi_ref, wo_ref, fr_ref, kf_ref, o_ref, hcat_ref):
    i = pl.program_id(0)
    rows_per_chunk = 256
    hi = lax.Precision.HIGHEST
    width = HYENA_WIDTH

    @pl.when(i == 0)
    def _():
        fr = fr_ref[...]

        def chunk(c, carry):
            r0 = pl.multiple_of(c * rows_per_chunk, rows_per_chunk)
            rows = pl.ds(r0, rows_per_chunk)
            hid = jnp.sin(fr * (jnp.dot(z_ref[rows, :], w1_ref[...], precision=hi,
                                        preferred_element_type=F32) + b1_ref[...]))
            for l in range(FILTER_INNER):
                hid = jnp.sin(fr * (jnp.dot(hid, wi_ref[l], precision=hi,
                                            preferred_element_type=F32) + bi_ref[l]))
            filt = jnp.dot(hid.astype(BF16), wo_ref[...].astype(BF16), preferred_element_type=F32)
            win = win_ref[rows, :]
            row = r0 + lax.broadcasted_iota(jnp.int32, (rows_per_chunk, 1), 0)
            h_fwd = filt[:, :width] * win
            h_bwd = jnp.where(row == 0, 0.0, filt[:, width:] * win)
            hcat_ref[rows, :width] = h_fwd.astype(BF16)
            hcat_ref[rows, width:] = h_bwd.astype(BF16)
            return carry

        lax.fori_loop(0, SEQ // rows_per_chunk, chunk, 0)

    def sums(g):
        sp = jnp.dot(kf_ref[g * FREQ_BLOCK:(g + 1) * FREQ_BLOCK, :], hcat_ref[...], preferred_element_type=F32)
        return sp[:, :width], sp[:, width:]

    first_row = (lax.broadcasted_iota(jnp.int32, (FREQ_BLOCK, 1), 0) == 0) & (i == 0)
    f, b = sums(0)
    re_even = f + b
    f, b = sums(1)
    nyquist = f[0:1] + b[0:1]
    o_ref[0] = re_even
    o_ref[1] = jnp.where(first_row, 0.0, b - f)
    o_ref[2] = jnp.where(first_row, nyquist, re_even)
    f, b = sums(2)
    o_ref[3] = f + b
    f, b = sums(3)
    o_ref[4] = b - f


def _hyena_filter_spectrum(z, window, w1, b1, w_inner, b_inner, w_out, freq, filter_dft):
    full = lambda a: pl.BlockSpec(a.shape, lambda i: (0,) * a.ndim)
    w1p = jnp.pad(w1, ((0, LANES - FILTER_EMB_DIM), (0, 0)))
    b1 = b1.reshape(1, FILTER_HIDDEN)
    b_inner = b_inner.reshape(FILTER_INNER, 1, FILTER_HIDDEN)
    freq = freq.reshape(1, FILTER_HIDDEN)
    ops = (z, window, w1p, b1, w_inner, b_inner, w_out, freq)
    return pl.pallas_call(
        _filter_kernel,
        grid=(N_FREQ_BLOCKS,),
        in_specs=[full(a) for a in ops] + [pl.BlockSpec((4 * FREQ_BLOCK, SEQ), lambda i: (i, 0))],
        out_specs=pl.BlockSpec((None, N_FILTER_SETS, FREQ_BLOCK, HYENA_WIDTH), lambda i: (i, 0, 0, 0)),
        out_shape=jax.ShapeDtypeStruct((N_FREQ_BLOCKS, N_FILTER_SETS, FREQ_BLOCK, HYENA_WIDTH), F32),
        scratch_shapes=[pltpu.VMEM((SEQ, 2 * HYENA_WIDTH), BF16)],
        compiler_params=_params(1),
        name="hyena_filter",
    )(*ops, filter_dft)


def _hy_proj_kernel(hn_ref, w_ref, cw_ref, cb_ref, x0_ref, uv_ref, wb_ref):
    _cast_weights_once(w_ref, wb_ref)
    n_chunks = SEQ // ROW_CHUNK
    pre = [jnp.dot(hn_ref[rc * ROW_CHUNK:(rc + 1) * ROW_CHUNK, :], wb_ref[...], preferred_element_type=F32)
           for rc in range(n_chunks)]
    w = cw_ref[...]
    bias = cb_ref[...]
    zeros = jnp.zeros((SUBLANES, HY_IN_WIDTH), F32)
    for rc in range(n_chunks):
        before = pre[rc - 1][ROW_CHUNK - SUBLANES:] if rc > 0 else zeros
        after = pre[rc + 1][:SUBLANES] if rc < n_chunks - 1 else zeros
        ext = ROW_CHUNK + SUBLANES
        prev = pltpu.roll(jnp.concatenate([before, pre[rc]], axis=0), 1, 0)[SUBLANES:]
        nxt = pltpu.roll(jnp.concatenate([pre[rc], after], axis=0), ext - 1, 0)[:ROW_CHUNK]
        u = bias + prev * w[0:1] + pre[rc] * w[1:2] + nxt * w[2:3]
        rows = slice(rc * ROW_CHUNK, (rc + 1) * ROW_CHUNK)
        x0_ref[rows, :] = u[:, :HYENA_WIDTH].astype(BF16)
        uv_ref[rows, :] = (u[:, HYENA_WIDTH:2 * HYENA_WIDTH] * u[:, 2 * HYENA_WIDTH:]).astype(BF16)


def _hy_proj(hn0, w_in, layer, conv_w, conv_b, batch):
    const = lambda a: pl.BlockSpec(a.shape, lambda b: (0,) * a.ndim)
    conv_b = conv_b.reshape(1, HY_IN_WIDTH)
    out_spec = pl.BlockSpec((SEQ, HYENA_WIDTH), lambda b: (b, 0))
    out_shape = jax.ShapeDtypeStruct((batch * SEQ, HYENA_WIDTH), BF16)
    return pl.pallas_call(
        _hy_proj_kernel,
        grid=(batch,),
        in_specs=[pl.BlockSpec((SEQ, D_MODEL), lambda b: (b, 0)), _weight_block(layer, HY_IN_WIDTH, 0),
                  const(conv_w), const(conv_b)],
        out_specs=[out_spec, out_spec],
        out_shape=[out_shape, out_shape],
        scratch_shapes=[pltpu.VMEM((D_MODEL, HY_IN_WIDTH), BF16)],
        compiler_params=_params(1),
        name="hy_proj",
    )(hn0, w_in, conv_w, conv_b)


def _cast_weights_once(w_ref, wb_ref):
    @pl.when(pl.program_id(0) == 0)
    def _():
        wb_ref[...] = w_ref[...].astype(BF16)


def _weight_block(layer, width, block):
    return pl.BlockSpec((None, D_MODEL, width), lambda b: (layer, 0, block), pipeline_mode=pl.Buffered(1))


def _qkv_proj_kernel(hn_ref, w_ref, rope_ref, o_ref, wb_ref):
    _cast_weights_once(w_ref, wb_ref)
    half = HEAD_DIM // 2
    q_scale = HEAD_DIM ** -0.5 * math.log2(math.e)
    for rc in range(SEQ // ROW_CHUNK):
        rows = slice(rc * ROW_CHUNK, (rc + 1) * ROW_CHUNK)
        acc = jnp.dot(hn_ref[rows, :], wb_ref[...], preferred_element_type=F32)
        cos, sa, sb = rope_ref[0, rows, :], rope_ref[1, rows, :], rope_ref[2, rows, :]
        for c in range(2 * ATTN_WIDTH // LANES):
            t = acc[:, c * LANES:(c + 1) * LANES]
            if c < ATTN_WIDTH // LANES:
                t = t * q_scale
            out = t * cos + pltpu.roll(t, LANES - half, 1) * sa + pltpu.roll(t, half, 1) * sb
            o_ref[rows, c * LANES:(c + 1) * LANES] = out.astype(BF16)
        o_ref[rows, 2 * ATTN_WIDTH:] = acc[:, 2 * ATTN_WIDTH:].astype(BF16)


def _qkv_proj(hn, w_in, layer, group, rope, batch):
    const = lambda a: pl.BlockSpec(a.shape, lambda b: (0,) * a.ndim)
    assert HY_IN_WIDTH == GROUP_WIDTH
    return pl.pallas_call(
        _qkv_proj_kernel,
        grid=(batch,),
        in_specs=[pl.BlockSpec((SEQ, D_MODEL), lambda b: (b, 0)), _weight_block(layer, GROUP_WIDTH, 1 + group),
                  const(rope)],
        out_specs=pl.BlockSpec((SEQ, GROUP_WIDTH), lambda b: (b, 0)),
        out_shape=jax.ShapeDtypeStruct((batch * SEQ, GROUP_WIDTH), BF16),
        scratch_shapes=[pltpu.VMEM((D_MODEL, GROUP_WIDTH), BF16)],
        compiler_params=_params(1),
        name="qkv_proj",
    )(hn, w_in, rope)


def _hyena_kernel(x0_ref, uv_ref, skip_ref, rev_ref, fe_ref, fo_ref, gs_ref, ga_ref, h_ref, o_ref):
    lo = uv_ref[:HALF, :].astype(F32)
    mirrored = jnp.dot(rev_ref[...], uv_ref[HALF:, :], preferred_element_type=F32)
    e = (lo + mirrored).astype(BF16)
    d = (lo - mirrored).astype(BF16)
    sym = anti = None
    for blk in range(N_FREQ_BLOCKS):
        span = slice(blk * 2 * FREQ_BLOCK, (blk + 1) * 2 * FREQ_BLOCK)
        from_e = jnp.dot(fe_ref[span, :], e, preferred_element_type=F32)
        from_d = jnp.dot(fo_ref[span, :], d, preferred_element_type=F32)
        a_e, b_o = from_e[:FREQ_BLOCK], from_e[FREQ_BLOCK:]
        a_o, b_e = from_d[:FREQ_BLOCK], from_d[FREQ_BLOCK:]
        yr_e = a_e * h_ref[blk, 0] + b_e * h_ref[blk, 1]
        yi_e = a_e * h_ref[blk, 1] - b_e * h_ref[blk, 2]
        yr_o = a_o * h_ref[blk, 3] + b_o * h_ref[blk, 4]
        yi_o = a_o * h_ref[blk, 4] - b_o * h_ref[blk, 3]
        part_s = jnp.dot(gs_ref[:, span], jnp.concatenate([yr_e, yi_o], axis=0).astype(BF16),
                         preferred_element_type=F32)
        part_a = jnp.dot(ga_ref[:, span], jnp.concatenate([yi_e, yr_o], axis=0).astype(BF16),
                         preferred_element_type=F32)
        sym = part_s if sym is None else sym + part_s
        anti = part_a if anti is None else anti + part_a
    skip = skip_ref[...]
    y_lo = sym + anti + skip * lo
    y_hi = (jnp.dot(rev_ref[...], (sym - anti).astype(BF16), preferred_element_type=F32)
            + skip * uv_ref[HALF:, :].astype(F32))
    o_ref[:HALF, :] = (x0_ref[:HALF, :].astype(F32) * y_lo).astype(BF16)
    o_ref[HALF:, :] = (x0_ref[HALF:, :].astype(F32) * y_hi).astype(BF16)


def _hyena(x0c, uv, skip, fold_dft, filters, batch):
    slab = pl.BlockSpec((SEQ, HYENA_WIDTH), lambda b: (b, 0))
    resident = lambda a: pl.BlockSpec(a.shape, lambda b: (0,) * a.ndim, pipeline_mode=pl.Buffered(1))
    fwd_e, fwd_o, inv_s, inv_a, rev = fold_dft
    tables = (rev, fwd_e, fwd_o, inv_s, inv_a, filters)
    return pl.pallas_call(
        _hyena_kernel,
        grid=(batch,),
        in_specs=[slab, slab, pl.BlockSpec((1, HYENA_WIDTH), lambda b: (0, 0))] + [resident(t) for t in tables],
        out_specs=slab,
        out_shape=jax.ShapeDtypeStruct((batch * SEQ, HYENA_WIDTH), BF16),
        compiler_params=_params(1),
        name="hyena",
    )(x0c, uv, skip.reshape(1, HYENA_WIDTH), *tables)


def _attention_kernel(cls_ref, bias_ref, *refs):
    qkv_refs = refs[:3 * N_GROUPS]
    o_ref, acc_ref, den_ref, max_ref = refs[3 * N_GROUPS:]
    lane = lax.broadcasted_iota(jnp.int32, (1, LANES), 1)
    first = lane < HEAD_DIM
    ones = jnp.ones((K_WINDOW, LANES), BF16)
    n_blocks = SEQ // Q_BLOCK

    for g, d in enumerate(DILATIONS):
        q_ref, k_ref, v_ref = qkv_refs[3 * g:3 * g + 3]
        n = SEQ // d

        def q_block(c, carry, g=g, d=d, n=n, q_ref=q_ref, k_ref=k_ref, v_ref=v_ref):
            q0 = pl.multiple_of(c * Q_BLOCK, Q_BLOCK)
            ws = pl.multiple_of(jnp.clip(q0 - RADIUS, 0, SEQ - K_WINDOW), RADIUS)
            q = q_ref[pl.ds(q0, Q_BLOCK), :]
            zero = jnp.zeros_like(q)
            q2 = jnp.concatenate([jnp.where(first, q, zero), jnp.where(first, zero, q)], axis=0)
            s = lax.dot_general(q2, k_ref[pl.ds(ws, K_WINDOW), :], (((1,), (1,)), ((), ())),
                                preferred_element_type=F32)
            bias = bias_ref[cls_ref[g * n_blocks + c]]
            s = s + jnp.concatenate([bias, bias], axis=0)
            m = jnp.max(s, axis=1, keepdims=True)
            p = jnp.exp2(s - m).astype(BF16)
            out = jnp.dot(p, jnp.concatenate([v_ref[pl.ds(ws, K_WINDOW), :], ones], axis=1),
                          preferred_element_type=F32)
            if d == 1:
                dst = pl.ds(q0, Q_BLOCK)
            else:
                dst = pl.ds(q0 // n + d * lax.rem(q0, n), Q_BLOCK, stride=d)
            acc_ref[g, dst, :] = jnp.where(first, out[:Q_BLOCK, :LANES], out[Q_BLOCK:, :LANES])
            den_ref[g, dst, :] = jnp.where(first, out[:Q_BLOCK, LANES:], out[Q_BLOCK:, LANES:])
            max_ref[g, dst, :] = jnp.where(first, m[:Q_BLOCK], m[Q_BLOCK:])
            return carry

        lax.fori_loop(0, n_blocks, q_block, 0, unroll=n_blocks)

    def combine(c, carry):
        rows = pl.ds(pl.multiple_of(c * Q_BLOCK, Q_BLOCK), Q_BLOCK)
        ms = [max_ref[g, rows, :] for g in range(N_GROUPS)]
        top = functools.reduce(jnp.maximum, ms)
        ws = [jnp.exp2(m - top) for m in ms]
        num = sum(ws[g] * acc_ref[g, rows, :] for g in range(N_GROUPS))
        den = sum(ws[g] * den_ref[g, rows, :] for g in range(N_GROUPS))
        o_ref[rows, :] = (num / den).astype(BF16)
        return carry

    lax.fori_loop(0, n_blocks, combine, 0, unroll=2)


def _attention(qkvs, batch):
    pairs = ATTN_WIDTH // LANES
    table, ids = (jnp.asarray(a) for a in _attention_masks())
    specs = [pl.BlockSpec(memory_space=pltpu.SMEM),
             pl.BlockSpec(table.shape, lambda b, hp: (0, 0, 0))]
    operands = []
    for g in range(N_GROUPS):
        for t in range(3):
            specs.append(pl.BlockSpec((SEQ, LANES), functools.partial(lambda b, hp, t: (b, t * pairs + hp), t=t)))
            operands.append(qkvs[g])
    group_rows = pltpu.VMEM((N_GROUPS, SEQ, LANES), F32)
    return pl.pallas_call(
        _attention_kernel,
        grid=(batch, pairs),
        in_specs=specs,
        out_specs=pl.BlockSpec((SEQ, LANES), lambda b, hp: (b, hp)),
        out_shape=jax.ShapeDtypeStruct((batch * SEQ, ATTN_WIDTH), BF16),
        scratch_shapes=[group_rows, group_rows, group_rows],
        compiler_params=_params(2),
        name="attention",
    )(ids, table, *operands)


def _merge_kernel(x_ref, hn_ref, yh_ref, ya_ref, wg_ref, ph_ref, pa_ref, wo_ref, o_ref):
    hn = hn_ref[...]
    g_hy = jax.nn.sigmoid(jnp.dot(hn, wg_ref[:, :D_MODEL], preferred_element_type=F32))
    merged = g_hy * jnp.dot(yh_ref[...], ph_ref[...], preferred_element_type=F32)
    g_att = jax.nn.sigmoid(jnp.dot(hn, wg_ref[:, D_MODEL:], preferred_element_type=F32))
    merged = merged + g_att * jnp.dot(ya_ref[...], pa_ref[...], preferred_element_type=F32)
    o_ref[...] = x_ref[...] + jnp.dot(merged.astype(BF16), wo_ref[...], preferred_element_type=F32)


def _merge(x2d, hn0, y_hy, y_att, w_gate, p_hy, p_att, w_o, rows):
    tm = 1024
    row = lambda w: pl.BlockSpec((tm, w), lambda i: (i, 0))
    const = lambda a: pl.BlockSpec(a.shape, lambda i: (0, 0))
    return pl.pallas_call(
        _merge_kernel,
        grid=(rows // tm,),
        in_specs=[row(D_MODEL), row(D_MODEL), row(HYENA_WIDTH), row(ATTN_WIDTH),
                  const(w_gate), const(p_hy), const(p_att), const(w_o)],
        out_specs=row(D_MODEL),
        out_shape=jax.ShapeDtypeStruct((rows, D_MODEL), F32),
        compiler_params=_params(1),
        name="merge",
    )(x2d, hn0, y_hy, y_att, w_gate, p_hy, p_att, w_o)


def _ffn_kernel(x_ref, g_ref, w1_ref, w2_ref, gn_ref, *rest, last):
    x = x_ref[...]
    hn = _rms(x, g_ref[...]).astype(BF16)
    chunk = 1024
    y = x
    for c in range(D_FF // chunk):
        h = jnp.dot(hn, w1_ref[:, c * chunk:(c + 1) * chunk], preferred_element_type=F32)
        h = jnp.square(jnp.maximum(h, 0.0)).astype(BF16)
        y = y + jnp.dot(h, w2_ref[c * chunk:(c + 1) * chunk, :], preferred_element_type=F32)
    if last:
        (o_ref,) = rest
        o_ref[...] = _rms(y, gn_ref[...])
    else:
        o_ref, o0_ref, o1_ref, o2_ref, a_ref, b_ref = rest
        o_ref[...] = y
        _store_row_orders(_rms(y, gn_ref[...]), o0_ref, o1_ref, o2_ref, a_ref, b_ref)


def _ffn(x2d, gain, w1, w2, gain_next, batch, last):
    rows = batch * SEQ
    row = pl.BlockSpec((ROW_TILE, D_MODEL), lambda i: (i, 0))
    const = lambda a: pl.BlockSpec(a.shape, lambda i: (0, 0))
    gain = gain.reshape(1, D_MODEL)
    gain_next = gain_next.reshape(1, D_MODEL)
    out_shape = [jax.ShapeDtypeStruct((rows, D_MODEL), F32)]
    out_specs = [row]
    scratch = []
    if not last:
        shapes, specs, scratch = _row_order_outputs(batch)
        out_shape += shapes
        out_specs += specs
    return pl.pallas_call(
        functools.partial(_ffn_kernel, last=last),
        grid=(rows // ROW_TILE,),
        in_specs=[row, const(gain), const(w1), const(w2), const(gain_next)],
        out_specs=out_specs,
        out_shape=out_shape,
        scratch_shapes=scratch,
        compiler_params=_params(1),
        name="ffn",
    )(x2d, gain, w1, w2, gain_next)


def kernel(x, norm_mix, w_in, conv_w, conv_b, filt_w1, filt_b1, filt_w_inner, filt_b_inner, filt_w_out,
           filt_freq, hy_skip, p_hy, p_att, w_o, norm_ffn, w_ff1, w_ff2, norm_final):
    batch, seq, d_model = x.shape
    assert (seq, d_model) == (SEQ, D_MODEL)
    depth = norm_mix.shape[0]
    rows = batch * seq
    bf = lambda a: a.astype(BF16)
    rope = tuple(jnp.asarray(t) for t in _rope_tables())
    fold_dft = tuple(bf(jnp.asarray(t)) for t in _fold_dft_tables())
    filter_dft = bf(jnp.asarray(_filter_dft_table()))
    z, window = (jnp.asarray(t) for t in _filter_tables())

    x2d = x.reshape(rows, d_model)
    hns = _norm_perm(x2d, norm_mix[0], batch)
    for l in range(depth):
        hns = [h.reshape(rows, d_model) for h in hns]
        filters = _hyena_filter_spectrum(z, window, filt_w1[l], filt_b1[l], filt_w_inner[l], filt_b_inner[l],
                                         filt_w_out[l], filt_freq[l], filter_dft)
        x0c, uv = _hy_proj(hns[0], w_in, l, conv_w[l], conv_b[l], batch)
        y_hy = _hyena(x0c, uv, hy_skip[l], fold_dft, filters, batch)
        qkvs = [_qkv_proj(hns[g], w_in, l, g, rope[g], batch) for g in range(N_GROUPS)]
        y_att = _attention(qkvs, batch)
        x2d = _merge(x2d, hns[0], y_hy, y_att, bf(w_in[l, :, HY_IN_WIDTH + QKV_WIDTH:]), bf(p_hy[l]), bf(p_att[l]),
                     bf(w_o[l]), rows)
        last = l == depth - 1
        outs = _ffn(x2d, norm_ffn[l], bf(w_ff1[l]), bf(w_ff2[l]), norm_final if last else norm_mix[l + 1], batch, last)
        x2d, hns = outs[0], outs[1:]
    return x2d.reshape(batch, seq, d_model)
```

```python
import functools
import math

import numpy as np
import jax
import jax.numpy as jnp
from jax import lax
from jax.experimental import pallas as pl
from jax.experimental.pallas import tpu as pltpu

F32 = jnp.float32
BF16 = jnp.bfloat16

D_MODEL = 1024
SEQ = 2048
HEAD_DIM = 64
ATTN_HEADS = 8
DILATIONS = (1, 4, 16)
RADIUS = 64
N_GROUPS = len(DILATIONS)
ATTN_WIDTH = ATTN_HEADS * HEAD_DIM
ROPE_THETA = 10000.0
NEG_INF = -1e30
HYENA_WIDTH = D_MODEL // 2
FILTER_BANDS = 16
FILTER_EMB_DIM = 1 + 2 * FILTER_BANDS
FILTER_HIDDEN = 64
FILTER_INNER = 2
DECAY_TARGET = 1e-2
FAST_DECAY_PCT = 0.3
SLOW_DECAY_PCT = 1.5
HY_IN_WIDTH = 3 * HYENA_WIDTH
GROUP_WIDTH = 3 * ATTN_WIDTH
QKV_WIDTH = N_GROUPS * GROUP_WIDTH
D_FF = 4 * D_MODEL
RMS_EPS = 1e-6

LANES = 128
SUBLANES = 8
N_SLABS = D_MODEL // LANES
N_FFT = 2 * SEQ
HALF = SEQ // 2
FREQ_BLOCK = 512
N_FREQ_BLOCKS = HALF // FREQ_BLOCK
N_FILTER_SETS = 5
Q_BLOCK = 128
K_WINDOW = 256
ROW_TILE = 512
ROW_CHUNK = 512
VMEM_LIMIT = 56 * 1024 * 1024

assert DILATIONS[0] == 1 and DILATIONS[2] == DILATIONS[1] ** 2


def _params(n_axes, vmem=VMEM_LIMIT):
    return pltpu.CompilerParams(dimension_semantics=("arbitrary",) * n_axes, vmem_limit_bytes=vmem)


def _perm_positions(dilation):
    n = SEQ // dilation
    r = np.arange(SEQ) // n
    m = np.arange(SEQ) % n
    return m * dilation + r


@functools.lru_cache(maxsize=None)
def _rope_tables():
    half = HEAD_DIM // 2
    inv_freq = ROPE_THETA ** (-np.arange(half, dtype=np.float64) / half)
    lane = np.arange(LANES)
    first_half = (lane % HEAD_DIM) < half
    tables = []
    for d in DILATIONS:
        ang = _perm_positions(d)[:, None].astype(np.float64) * inv_freq[None, :]
        cos_l = np.cos(ang)[:, lane % half]
        sin_l = np.sin(ang)[:, lane % half]
        sa = np.where(first_half[None, :], -sin_l, 0.0)
        sb = np.where(first_half[None, :], 0.0, sin_l)
        tables.append(np.stack([cos_l, sa, sb]).astype(np.float32))
    return tuple(tables)


def _blocked(rows_a, rows_b, axis):
    parts = []
    for b in range(N_FREQ_BLOCKS):
        sl = [slice(None)] * rows_a.ndim
        sl[axis] = slice(b * FREQ_BLOCK, (b + 1) * FREQ_BLOCK)
        parts += [rows_a[tuple(sl)], rows_b[tuple(sl)]]
    return np.concatenate(parts, axis=axis)


@functools.lru_cache(maxsize=None)
def _fold_dft_tables():
    j = np.arange(HALF, dtype=np.int64)[:, None]
    n = np.arange(HALF, dtype=np.int64)[None, :]
    ang = lambda k: ((k * (2 * n + 1)) % (2 * N_FFT)).astype(np.float64) * (2.0 * math.pi / (2 * N_FFT))
    ce, se, co, so = np.cos(ang(2 * j)), np.sin(ang(2 * j)), np.cos(ang(2 * j + 1)), np.sin(ang(2 * j + 1))
    se[0, :] = (-1.0) ** np.arange(HALF)
    w_e = np.full((HALF, 1), 2.0 / N_FFT)
    w_e[0] = 1.0 / N_FFT
    w_o = 2.0 / N_FFT
    fwd_e = _blocked(ce, so, 0)
    fwd_o = _blocked(co, se, 0)
    inv_s = _blocked((w_e * ce).T, (-w_o * so).T, 1)
    inv_a = _blocked((-w_e * se).T, (w_o * co).T, 1)
    rev = np.eye(HALF)[::-1]
    return tuple(np.ascontiguousarray(t).astype(np.float32) for t in (fwd_e, fwd_o, inv_s, inv_a, rev))


@functools.lru_cache(maxsize=None)
def _filter_dft_table():
    j = np.arange(HALF, dtype=np.int64)[:, None]
    lag = np.arange(SEQ, dtype=np.int64)[None, :]
    ang = lambda k: ((k * lag) % N_FFT).astype(np.float64) * (2.0 * math.pi / N_FFT)
    ce, se, co, so = np.cos(ang(2 * j)), np.sin(ang(2 * j)), np.cos(ang(2 * j + 1)), np.sin(ang(2 * j + 1))
    se[0, :] = (-1.0) ** np.arange(SEQ)
    parts = []
    for b in range(N_FREQ_BLOCKS):
        rows = slice(b * FREQ_BLOCK, (b + 1) * FREQ_BLOCK)
        parts += [ce[rows], se[rows], co[rows], so[rows]]
    return np.concatenate(parts, axis=0).astype(np.float32)


@functools.lru_cache(maxsize=None)
def _filter_tables():
    n = np.arange(SEQ, dtype=np.float64)
    t = n / max(SEQ - 1, 1)
    bands = np.linspace(1e-4, FILTER_BANDS - 1, FILTER_BANDS)
    ang = (2.0 * math.pi / SEQ) * n[:, None] * bands[None, :]
    z = np.concatenate([t[:, None], np.cos(ang), -np.sin(ang)], axis=-1)
    z = np.pad(z, ((0, 0), (0, LANES - FILTER_EMB_DIM)))
    max_decay = math.log(DECAY_TARGET) / FAST_DECAY_PCT
    min_decay = math.log(DECAY_TARGET) / SLOW_DECAY_PCT
    deltas = np.abs(np.linspace(min_decay, max_decay, HYENA_WIDTH))
    window = np.exp(-t[:, None] * deltas[None, :])
    return z.astype(np.float32), window.astype(np.float32)


@functools.lru_cache(maxsize=None)
def _attention_masks():
    i = np.arange(Q_BLOCK)[:, None]
    j = np.arange(K_WINDOW)[None, :]
    classes, ids = {}, []
    for d in DILATIONS:
        n = SEQ // d
        for c in range(SEQ // Q_BLOCK):
            q0 = c * Q_BLOCK
            ws = min(max(q0 - RADIUS, 0), SEQ - K_WINDOW)
            seg = (q0 // n) * n
            kpos = ws + j
            valid = (np.abs(kpos - (q0 + i)) <= RADIUS) & (kpos >= seg) & (kpos < seg + n)
            bias = np.where(valid, 0.0, NEG_INF).astype(np.float32)
            ids.append(classes.setdefault(bias.tobytes(), len(classes)))
    table = np.stack([np.frombuffer(b, np.float32).reshape(Q_BLOCK, K_WINDOW) for b in classes])
    return table, np.asarray(ids, np.int32)


def _rms(x, gain):
    return x * lax.rsqrt(jnp.mean(x * x, axis=-1, keepdims=True) + RMS_EPS) * gain


def _store_row_orders(hn, o0_ref, o1_ref, o2_ref, a_ref, b_ref):
    d1, d2 = DILATIONS[1], DILATIONS[2]
    n1, n2 = ROW_TILE // d1, ROW_TILE // d2
    o0_ref[...] = hn.astype(BF16)
    for s in range(N_SLABS):
        a_ref[s] = hn[:, s * LANES:(s + 1) * LANES]
    for r in range(d1):
        v = jnp.concatenate([a_ref[s, pl.ds(r, n1, stride=d1), :] for s in range(N_SLABS)], axis=1)
        o1_ref[r] = v.astype(BF16)
        for s in range(N_SLABS):
            b_ref[s, r * n1:(r + 1) * n1, :] = v[:, s * LANES:(s + 1) * LANES]
    for r in range(d2):
        src = pl.ds((r % d1) * n1 + r // d1, n2, stride=d1)
        v = jnp.concatenate([b_ref[s, src, :] for s in range(N_SLABS)], axis=1)
        o2_ref[r] = v.astype(BF16)


def _row_order_outputs(batch):
    d1, d2 = DILATIONS[1], DILATIONS[2]
    tiles = SEQ // ROW_TILE
    shapes = [jax.ShapeDtypeStruct((batch * SEQ, D_MODEL), BF16),
              jax.ShapeDtypeStruct((batch, d1, SEQ // d1, D_MODEL), BF16),
              jax.ShapeDtypeStruct((batch, d2, SEQ // d2, D_MODEL), BF16)]
    specs = [pl.BlockSpec((ROW_TILE, D_MODEL), lambda i: (i, 0)),
             pl.BlockSpec((None, d1, ROW_TILE // d1, D_MODEL), lambda i: (i // tiles, 0, i % tiles, 0)),
             pl.BlockSpec((None, d2, ROW_TILE // d2, D_MODEL), lambda i: (i // tiles, 0, i % tiles, 0))]
    scratch = [pltpu.VMEM((N_SLABS, ROW_TILE, LANES), F32), pltpu.VMEM((N_SLABS, ROW_TILE, LANES), F32)]
    return shapes, specs, scratch


def _norm_perm_kernel(x_ref, g_ref, o0_ref, o1_ref, o2_ref, a_ref, b_ref):
    _store_row_orders(_rms(x_ref[...], g_ref[...]), o0_ref, o1_ref, o2_ref, a_ref, b_ref)


def _norm_perm(x2d, gain, batch):
    shapes, specs, scratch = _row_order_outputs(batch)
    return pl.pallas_call(
        _norm_perm_kernel,
        grid=(batch * SEQ // ROW_TILE,),
        in_specs=[pl.BlockSpec((ROW_TILE, D_MODEL), lambda i: (i, 0)), pl.BlockSpec((1, D_MODEL), lambda i: (0, 0))],
        out_specs=specs,
        out_shape=shapes,
        scratch_shapes=scratch,
        compiler_params=_params(1),
        name="norm_perm",
    )(x2d, gain.reshape(1, D_MODEL))


def _filter_kernel(z_ref, win_ref, w1_ref, b1_ref, wi_ref, bi_ref, wo_ref, fr_ref, kf_ref, o_ref, hcat_ref):
    i = pl.program_id(0)
    rows_per_chunk = 256
    hi = lax.Precision.HIGHEST
    width = HYENA_WIDTH

    @pl.when(i == 0)
    def _():
        fr = fr_ref[...]

        def chunk(c, carry):
            r0 = pl.multiple_of(c * rows_per_chunk, rows_per_chunk)
            rows = pl.ds(r0, rows_per_chunk)
            hid = jnp.sin(fr * (jnp.dot(z_ref[rows, :], w1_ref[...], precision=hi,
                                        preferred_element_type=F32) + b1_ref[...]))
            for l in range(FILTER_INNER):
                hid = jnp.sin(fr * (jnp.dot(hid, wi_ref[l], precision=hi,
                                            preferred_element_type=F32) + bi_ref[l]))
            filt = jnp.dot(hid.astype(BF16), wo_ref[...].astype(BF16), preferred_element_type=F32)
            win = win_ref[rows, :]
            row = r0 + lax.broadcasted_iota(jnp.int32, (rows_per_chunk, 1), 0)
            h_fwd = filt[:, :width] * win
            h_bwd = jnp.where(row == 0, 0.0, filt[:, width:] * win)
            hcat_ref[rows, :width] = h_fwd.astype(BF16)
            hcat_ref[rows, width:] = h_bwd.astype(BF16)
            return carry

        lax.fori_loop(0, SEQ // rows_per_chunk, chunk, 0)

    def sums(g):
        sp = jnp.dot(kf_ref[g * FREQ_BLOCK:(g + 1) * FREQ_BLOCK, :], hcat_ref[...], preferred_element_type=F32)
        return sp[:, :width], sp[:, width:]

    first_row = (lax.broadcasted_iota(jnp.int32, (FREQ_BLOCK, 1), 0) == 0) & (i == 0)
    f, b = sums(0)
    re_even = f + b
    f, b = sums(1)
    nyquist = f[0:1] + b[0:1]
    o_ref[0] = re_even
    o_ref[1] = jnp.where(first_row, 0.0, b - f)
    o_ref[2] = jnp.where(first_row, nyquist, re_even)
    f, b = sums(2)
    o_ref[3] = f + b
    f, b = sums(3)
    o_ref[4] = b - f


def _hyena_filter_spectrum(z, window, w1, b1, w_inner, b_inner, w_out, freq, filter_dft):
    full = lambda a: pl.BlockSpec(a.shape, lambda i: (0,) * a.ndim)
    w1p = jnp.pad(w1, ((0, LANES - FILTER_EMB_DIM), (0, 0)))
    b1 = b1.reshape(1, FILTER_HIDDEN)
    b_inner = b_inner.reshape(FILTER_INNER, 1, FILTER_HIDDEN)
    freq = freq.reshape(1, FILTER_HIDDEN)
    ops = (z, window, w1p, b1, w_inner, b_inner, w_out, freq)
    return pl.pallas_call(
        _filter_kernel,
        grid=(N_FREQ_BLOCKS,),
        in_specs=[full(a) for a in ops] + [pl.BlockSpec((4 * FREQ_BLOCK, SEQ), lambda i: (i, 0))],
        out_specs=pl.BlockSpec((None, N_FILTER_SETS, FREQ_BLOCK, HYENA_WIDTH), lambda i: (i, 0, 0, 0)),
        out_shape=jax.ShapeDtypeStruct((N_FREQ_BLOCKS, N_FILTER_SETS, FREQ_BLOCK, HYENA_WIDTH), F32),
        scratch_shapes=[pltpu.VMEM((SEQ, 2 * HYENA_WIDTH), BF16)],
        compiler_params=_params(1),
        name="hyena_filter",
    )(*ops, filter_dft)


def _hy_proj_kernel(hn_ref, w_ref, cw_ref, cb_ref, x0_ref, uv_ref, wb_ref):
    _cast_weights_once(w_ref, wb_ref)
    n_chunks = SEQ // ROW_CHUNK
    pre = [jnp.dot(hn_ref[rc * ROW_CHUNK:(rc + 1) * ROW_CHUNK, :], wb_ref[...], preferred_element_type=F32)
           for rc in range(n_chunks)]
    w = cw_ref[...]
    bias = cb_ref[...]
    zeros = jnp.zeros((SUBLANES, HY_IN_WIDTH), F32)
    for rc in range(n_chunks):
        before = pre[rc - 1][ROW_CHUNK - SUBLANES:] if rc > 0 else zeros
        after = pre[rc + 1][:SUBLANES] if rc < n_chunks - 1 else zeros
        ext = ROW_CHUNK + SUBLANES
        prev = pltpu.roll(jnp.concatenate([before, pre[rc]], axis=0), 1, 0)[SUBLANES:]
        nxt = pltpu.roll(jnp.concatenate([pre[rc], after], axis=0), ext - 1, 0)[:ROW_CHUNK]
        u = bias + prev * w[0:1] + pre[rc] * w[1:2] + nxt * w[2:3]
        rows = slice(rc * ROW_CHUNK, (rc + 1) * ROW_CHUNK)
        x0_ref[rows, :] = u[:, :HYENA_WIDTH].astype(BF16)
        uv_ref[rows, :] = (u[:, HYENA_WIDTH:2 * HYENA_WIDTH] * u[:, 2 * HYENA_WIDTH:]).astype(BF16)


def _hy_proj(hn0, w_in, layer, conv_w, conv_b, batch):
    const = lambda a: pl.BlockSpec(a.shape, lambda b: (0,) * a.ndim)
    conv_b = conv_b.reshape(1, HY_IN_WIDTH)
    out_spec = pl.BlockSpec((SEQ, HYENA_WIDTH), lambda b: (b, 0))
    out_shape = jax.ShapeDtypeStruct((batch * SEQ, HYENA_WIDTH), BF16)
    return pl.pallas_call(
        _hy_proj_kernel,
        grid=(batch,),
        in_specs=[pl.BlockSpec((SEQ, D_MODEL), lambda b: (b, 0)), _weight_block(layer, HY_IN_WIDTH, 0),
                  const(conv_w), const(conv_b)],
        out_specs=[out_spec, out_spec],
        out_shape=[out_shape, out_shape],
        scratch_shapes=[pltpu.VMEM((D_MODEL, HY_IN_WIDTH), BF16)],
        compiler_params=_params(1),
        name="hy_proj",
    )(hn0, w_in, conv_w, conv_b)


def _cast_weights_once(w_ref, wb_ref):
    @pl.when(pl.program_id(0) == 0)
    def _():
        wb_ref[...] = w_ref[...].astype(BF16)


def _weight_block(layer, width, block):
    return pl.BlockSpec((None, D_MODEL, width), lambda b: (layer, 0, block), pipeline_mode=pl.Buffered(1))


def _qkv_proj_kernel(hn_ref, w_ref, rope_ref, o_ref, wb_ref):
    _cast_weights_once(w_ref, wb_ref)
    half = HEAD_DIM // 2
    q_scale = HEAD_DIM ** -0.5 * math.log2(math.e)
    for rc in range(SEQ // ROW_CHUNK):
        rows = slice(rc * ROW_CHUNK, (rc + 1) * ROW_CHUNK)
        acc = jnp.dot(hn_ref[rows, :], wb_ref[...], preferred_element_type=F32)
        cos, sa, sb = rope_ref[0, rows, :], rope_ref[1, rows, :], rope_ref[2, rows, :]
        for c in range(2 * ATTN_WIDTH // LANES):
            t = acc[:, c * LANES:(c + 1) * LANES]
            if c < ATTN_WIDTH // LANES:
                t = t * q_scale
            out = t * cos + pltpu.roll(t, LANES - half, 1) * sa + pltpu.roll(t, half, 1) * sb
            o_ref[rows, c * LANES:(c + 1) * LANES] = out.astype(BF16)
        o_ref[rows, 2 * ATTN_WIDTH:] = acc[:, 2 * ATTN_WIDTH:].astype(BF16)


def _qkv_proj(hn, w_in, layer, group, rope, batch):
    const = lambda a: pl.BlockSpec(a.shape, lambda b: (0,) * a.ndim)
    assert HY_IN_WIDTH == GROUP_WIDTH
    return pl.pallas_call(
        _qkv_proj_kernel,
        grid=(batch,),
        in_specs=[pl.BlockSpec((SEQ, D_MODEL), lambda b: (b, 0)), _weight_block(layer, GROUP_WIDTH, 1 + group),
                  const(rope)],
        out_specs=pl.BlockSpec((SEQ, GROUP_WIDTH), lambda b: (b, 0)),
        out_shape=jax.ShapeDtypeStruct((batch * SEQ, GROUP_WIDTH), BF16),
        scratch_shapes=[pltpu.VMEM((D_MODEL, GROUP_WIDTH), BF16)],
        compiler_params=_params(1),
        name="qkv_proj",
    )(hn, w_in, rope)


def _hyena_kernel(x0_ref, uv_ref, skip_ref, rev_ref, fe_ref, fo_ref, gs_ref, ga_ref, h_ref, o_ref):
    lo = uv_ref[:HALF, :].astype(F32)
    mirrored = jnp.dot(rev_ref[...], uv_ref[HALF:, :], preferred_element_type=F32)
    e = (lo + mirrored).astype(BF16)
    d = (lo - mirrored).astype(BF16)
    sym = anti = None
    for blk in range(N_FREQ_BLOCKS):
        span = slice(blk * 2 * FREQ_BLOCK, (blk + 1) * 2 * FREQ_BLOCK)
        from_e = jnp.dot(fe_ref[span, :], e, preferred_element_type=F32)
        from_d = jnp.dot(fo_ref[span, :], d, preferred_element_type=F32)
        a_e, b_o = from_e[:FREQ_BLOCK], from_e[FREQ_BLOCK:]
        a_o, b_e = from_d[:FREQ_BLOCK], from_d[FREQ_BLOCK:]
        yr_e = a_e * h_ref[blk, 0] + b_e * h_ref[blk, 1]
        yi_e = a_e * h_ref[blk, 1] - b_e * h_ref[blk, 2]
        yr_o = a_o * h_ref[blk, 3] + b_o * h_ref[blk, 4]
        yi_o = a_o * h_ref[blk, 4] - b_o * h_ref[blk, 3]
        part_s = jnp.dot(gs_ref[:, span], jnp.concatenate([yr_e, yi_o], axis=0).astype(BF16),
                         preferred_element_type=F32)
        part_a = jnp.dot(ga_ref[:, span], jnp.concatenate([yi_e, yr_o], axis=0).astype(BF16),
                         preferred_element_type=F32)
        sym = part_s if sym is None else sym + part_s
        anti = part_a if anti is None else anti + part_a
    skip = skip_ref[...]
    y_lo = sym + anti + skip * lo
    y_hi = (jnp.dot(rev_ref[...], (sym - anti).astype(BF16), preferred_element_type=F32)
            + skip * uv_ref[HALF:, :].astype(F32))
    o_ref[:HALF, :] = (x0_ref[:HALF, :].astype(F32) * y_lo).astype(BF16)
    o_ref[HALF:, :] = (x0_ref[HALF:, :].astype(F32) * y_hi).astype(BF16)


def _hyena(x0c, uv, skip, fold_dft, filters, batch):
    slab = pl.BlockSpec((SEQ, HYENA_WIDTH), lambda b: (b, 0))
    resident = lambda a: pl.BlockSpec(a.shape, lambda b: (0,) * a.ndim, pipeline_mode=pl.Buffered(1))
    fwd_e, fwd_o, inv_s, inv_a, rev = fold_dft
    tables = (rev, fwd_e, fwd_o, inv_s, inv_a, filters)
    return pl.pallas_call(
        _hyena_kernel,
        grid=(batch,),
        in_specs=[slab, slab, pl.BlockSpec((1, HYENA_WIDTH), lambda b: (0, 0))] + [resident(t) for t in tables],
        out_specs=slab,
        out_shape=jax.ShapeDtypeStruct((batch * SEQ, HYENA_WIDTH), BF16),
        compiler_params=_params(1),
        name="hyena",
    )(x0c, uv, skip.reshape(1, HYENA_WIDTH), *tables)


def _attention_kernel(cls_ref, bias_ref, *refs):
    qkv_refs = refs[:3 * N_GROUPS]
    o_ref, acc_ref, den_ref, max_ref = refs[3 * N_GROUPS:]
    lane = lax.broadcasted_iota(jnp.int32, (1, LANES), 1)
    first = lane < HEAD_DIM
    ones = jnp.ones((K_WINDOW, LANES), BF16)
    n_blocks = SEQ // Q_BLOCK

    for g, d in reversed(list(enumerate(DILATIONS))):
        q_ref, k_ref, v_ref = qkv_refs[3 * g:3 * g + 3]
        n = SEQ // d

        def q_block(c, carry, g=g, d=d, n=n, q_ref=q_ref, k_ref=k_ref, v_ref=v_ref):
            q0 = pl.multiple_of(c * Q_BLOCK, Q_BLOCK)
            ws = pl.multiple_of(jnp.clip(q0 - RADIUS, 0, SEQ - K_WINDOW), RADIUS)
            q = q_ref[pl.ds(q0, Q_BLOCK), :]
            zero = jnp.zeros_like(q)
            q2 = jnp.concatenate([jnp.where(first, q, zero), jnp.where(first, zero, q)], axis=0)
            s = lax.dot_general(q2, k_ref[pl.ds(ws, K_WINDOW), :], (((1,), (1,)), ((), ())),
                                preferred_element_type=F32)
            bias = bias_ref[cls_ref[g * n_blocks + c]]
            s = s + jnp.concatenate([bias, bias], axis=0)
            m = jnp.max(s, axis=1, keepdims=True)
            p = jnp.exp2(s - m).astype(BF16)
            out = jnp.dot(p, jnp.concatenate([v_ref[pl.ds(ws, K_WINDOW), :], ones], axis=1),
                          preferred_element_type=F32)
            acc = jnp.where(first, out[:Q_BLOCK, :LANES], out[Q_BLOCK:, :LANES])
            den = jnp.where(first, out[:Q_BLOCK, LANES:], out[Q_BLOCK:, LANES:])
            top = jnp.where(first, m[:Q_BLOCK], m[Q_BLOCK:]) + jnp.zeros((Q_BLOCK, LANES), F32)
            if d > 1:
                dst = pl.ds(q0 // n + d * lax.rem(q0, n), Q_BLOCK, stride=d)
                acc_ref[g - 1, dst, :] = acc
                den_ref[g - 1, dst, :] = den
                max_ref[g - 1, dst, :] = top
            else:
                rows = pl.ds(q0, Q_BLOCK)
                ms = [top] + [max_ref[o, rows, :] for o in range(N_GROUPS - 1)]
                accs = [acc] + [acc_ref[o, rows, :] for o in range(N_GROUPS - 1)]
                dens = [den] + [den_ref[o, rows, :] for o in range(N_GROUPS - 1)]
                peak = functools.reduce(jnp.maximum, ms)
                wts = [jnp.exp2(mm - peak) for mm in ms]
                num = sum(w * a for w, a in zip(wts, accs))
                tot = sum(w * dd for w, dd in zip(wts, dens))
                o_ref[rows, :] = (num / tot).astype(BF16)
            return carry

        lax.fori_loop(0, n_blocks, q_block, 0, unroll=n_blocks)


def _attention(qkvs, batch):
    pairs = ATTN_WIDTH // LANES
    table, ids = (jnp.asarray(a) for a in _attention_masks())
    specs = [pl.BlockSpec(memory_space=pltpu.SMEM),
             pl.BlockSpec(table.shape, lambda b, hp: (0, 0, 0))]
    operands = []
    for g in range(N_GROUPS):
        for t in range(3):
            specs.append(pl.BlockSpec((SEQ, LANES), functools.partial(lambda b, hp, t: (b, t * pairs + hp), t=t)))
            operands.append(qkvs[g])
    group_rows = pltpu.VMEM((N_GROUPS - 1, SEQ, LANES), F32)
    return pl.pallas_call(
        _attention_kernel,
        grid=(batch, pairs),
        in_specs=specs,
        out_specs=pl.BlockSpec((SEQ, LANES), lambda b, hp: (b, hp)),
        out_shape=jax.ShapeDtypeStruct((batch * SEQ, ATTN_WIDTH), BF16),
        scratch_shapes=[group_rows, group_rows, group_rows],
        compiler_params=_params(2),
        name="attention",
    )(ids, table, *operands)


def _merge_kernel(x_ref, hn_ref, yh_ref, ya_ref, wg_ref, ph_ref, pa_ref, wo_ref, o_ref):
    hn = hn_ref[...]
    g_hy = jax.nn.sigmoid(jnp.dot(hn, wg_ref[:, :D_MODEL], preferred_element_type=F32))
    merged = g_hy * jnp.dot(yh_ref[...], ph_ref[...], preferred_element_type=F32)
    g_att = jax.nn.sigmoid(jnp.dot(hn, wg_ref[:, D_MODEL:], preferred_element_type=F32))
    merged = merged + g_att * jnp.dot(ya_ref[...], pa_ref[...], preferred_element_type=F32)
    o_ref[...] = x_ref[...] + jnp.dot(merged.astype(BF16), wo_ref[...], preferred_element_type=F32)


def _merge(x2d, hn0, y_hy, y_att, w_gate, p_hy, p_att, w_o, rows):
    tm = 1024
    row = lambda w: pl.BlockSpec((tm, w), lambda i: (i, 0))
    const = lambda a: pl.BlockSpec(a.shape, lambda i: (0, 0))
    return pl.pallas_call(
        _merge_kernel,
        grid=(rows // tm,),
        in_specs=[row(D_MODEL), row(D_MODEL), row(HYENA_WIDTH), row(ATTN_WIDTH),
                  const(w_gate), const(p_hy), const(p_att), const(w_o)],
        out_specs=row(D_MODEL),
        out_shape=jax.ShapeDtypeStruct((rows, D_MODEL), F32),
        compiler_params=_params(1),
        name="merge",
    )(x2d, hn0, y_hy, y_att, w_gate, p_hy, p_att, w_o)


def _ffn_kernel(x_ref, g_ref, w1_ref, w2_ref, gn_ref, *rest, last):
    x = x_ref[...]
    hn = _rms(x, g_ref[...]).astype(BF16)
    chunk = 1024
    y = x
    for c in range(D_FF // chunk):
        h = jnp.dot(hn, w1_ref[:, c * chunk:(c + 1) * chunk], preferred_element_type=F32)
        h = jnp.square(jnp.maximum(h, 0.0)).astype(BF16)
        y = y + jnp.dot(h, w2_ref[c * chunk:(c + 1) * chunk, :], preferred_element_type=F32)
    if last:
        (o_ref,) = rest
        o_ref[...] = _rms(y, gn_ref[...])
    else:
        o_ref, o0_ref, o1_ref, o2_ref, a_ref, b_ref = rest
        o_ref[...] = y
        _store_row_orders(_rms(y, gn_ref[...]), o0_ref, o1_ref, o2_ref, a_ref, b_ref)


def _ffn(x2d, gain, w1, w2, gain_next, batch, last):
    rows = batch * SEQ
    row = pl.BlockSpec((ROW_TILE, D_MODEL), lambda i: (i, 0))
    const = lambda a: pl.BlockSpec(a.shape, lambda i: (0, 0))
    gain = gain.reshape(1, D_MODEL)
    gain_next = gain_next.reshape(1, D_MODEL)
    out_shape = [jax.ShapeDtypeStruct((rows, D_MODEL), F32)]
    out_specs = [row]
    scratch = []
    if not last:
        shapes, specs, scratch = _row_order_outputs(batch)
        out_shape += shapes
        out_specs += specs
    return pl.pallas_call(
        functools.partial(_ffn_kernel, last=last),
        grid=(rows // ROW_TILE,),
        in_specs=[row, const(gain), const(w1), const(w2), const(gain_next)],
        out_specs=out_specs,
        out_shape=out_shape,
        scratch_shapes=scratch,
        compiler_params=_params(1),
        name="ffn",
    )(x2d, gain, w1, w2, gain_next)


def kernel(x, norm_mix, w_in, conv_w, conv_b, filt_w1, filt_b1, filt_w_inner, filt_b_inner, filt_w_out,
           filt_freq, hy_skip, p_hy, p_att, w_o, norm_ffn, w_ff1, w_ff2, norm_final):
    batch, seq, d_model = x.shape
    assert (seq, d_model) == (SEQ, D_MODEL)
    depth = norm_mix.shape[0]
    rows = batch * seq
    bf = lambda a: a.astype(BF16)
    rope = tuple(jnp.asarray(t) for t in _rope_tables())
    fold_dft = tuple(bf(jnp.asarray(t)) for t in _fold_dft_tables())
    filter_dft = bf(jnp.asarray(_filter_dft_table()))
    z, window = (jnp.asarray(t) for t in _filter_tables())

    x2d = x.reshape(rows, d_model)
    hns = _norm_perm(x2d, norm_mix[0], batch)
    for l in range(depth):
        hns = [h.reshape(rows, d_model) for h in hns]
        filters = _hyena_filter_spectrum(z, window, filt_w1[l], filt_b1[l], filt_w_inner[l], filt_b_inner[l],
                                         filt_w_out[l], filt_freq[l], filter_dft)
        x0c, uv = _hy_proj(hns[0], w_in, l, conv_w[l], conv_b[l], batch)
        y_hy = _hyena(x0c, uv, hy_skip[l], fold_dft, filters, batch)
        qkvs = [_qkv_proj(hns[g], w_in, l, g, rope[g], batch) for g in range(N_GROUPS)]
        y_att = _attention(qkvs, batch)
        x2d = _merge(x2d, hns[0], y_hy, y_att, bf(w_in[l, :, HY_IN_WIDTH + QKV_WIDTH:]), bf(p_hy[l]), bf(p_att[l]),
                     bf(w_o[l]), rows)
        last = l == depth - 1
        outs = _ffn(x2d, norm_ffn[l], bf(w_ff1[l]), bf(w_ff2[l]), norm_final if last else norm_mix[l + 1], batch, last)
        x2d, hns = outs[0], outs[1:]
    return x2d.reshape(batch, seq, d_model)
```

```python
import functools
import math

import numpy as np
import jax
import jax.numpy as jnp
from jax import lax
from jax.experimental import pallas as pl
from jax.experimental.pallas import tpu as pltpu

F32 = jnp.float32
BF16 = jnp.bfloat16

D_MODEL = 1024
SEQ = 2048
HEAD_DIM = 64
ATTN_HEADS = 8
DILATIONS = (1, 4, 16)
RADIUS = 64
N_GROUPS = len(DILATIONS)
ATTN_WIDTH = ATTN_HEADS * HEAD_DIM
ROPE_THETA = 10000.0
NEG_INF = -1e30
HYENA_WIDTH = D_MODEL // 2
FILTER_BANDS = 16
FILTER_EMB_DIM = 1 + 2 * FILTER_BANDS
FILTER_HIDDEN = 64
FILTER_INNER = 2
DECAY_TARGET = 1e-2
FAST_DECAY_PCT = 0.3
SLOW_DECAY_PCT = 1.5
HY_IN_WIDTH = 3 * HYENA_WIDTH
GROUP_WIDTH = 3 * ATTN_WIDTH
QKV_WIDTH = N_GROUPS * GROUP_WIDTH
D_FF = 4 * D_MODEL
RMS_EPS = 1e-6

LANES = 128
SUBLANES = 8
N_SLABS = D_MODEL // LANES
N_FFT = 2 * SEQ
HALF = SEQ // 2
FREQ_BLOCK = 512
N_FREQ_BLOCKS = HALF // FREQ_BLOCK
N_FILTER_SETS = 5
Q_BLOCK = 128
K_WINDOW = 256
ROW_TILE = 512
ROW_CHUNK = 512
VMEM_LIMIT = 56 * 1024 * 1024

assert DILATIONS[0] == 1 and DILATIONS[2] == DILATIONS[1] ** 2


def _params(n_axes, vmem=VMEM_LIMIT):
    return pltpu.CompilerParams(dimension_semantics=("arbitrary",) * n_axes, vmem_limit_bytes=vmem)


def _perm_positions(dilation):
    n = SEQ // dilation
    r = np.arange(SEQ) // n
    m = np.arange(SEQ) % n
    return m * dilation + r


@functools.lru_cache(maxsize=None)
def _rope_tables():
    half = HEAD_DIM // 2
    inv_freq = ROPE_THETA ** (-np.arange(half, dtype=np.float64) / half)
    lane = np.arange(LANES)
    first_half = (lane % HEAD_DIM) < half
    tables = []
    for d in DILATIONS:
        ang = _perm_positions(d)[:, None].astype(np.float64) * inv_freq[None, :]
        cos_l = np.cos(ang)[:, lane % half]
        sin_l = np.sin(ang)[:, lane % half]
        sa = np.where(first_half[None, :], -sin_l, 0.0)
        sb = np.where(first_half[None, :], 0.0, sin_l)
        tables.append(np.stack([cos_l, sa, sb]).astype(np.float32))
    return tuple(tables)


def _blocked(rows_a, rows_b, axis):
    parts = []
    for b in range(N_FREQ_BLOCKS):
        sl = [slice(None)] * rows_a.ndim
        sl[axis] = slice(b * FREQ_BLOCK, (b + 1) * FREQ_BLOCK)
        parts += [rows_a[tuple(sl)], rows_b[tuple(sl)]]
    return np.concatenate(parts, axis=axis)


@functools.lru_cache(maxsize=None)
def _fold_dft_tables():
    j = np.arange(HALF, dtype=np.int64)[:, None]
    n = np.arange(HALF, dtype=np.int64)[None, :]
    ang = lambda k: ((k * (2 * n + 1)) % (2 * N_FFT)).astype(np.float64) * (2.0 * math.pi / (2 * N_FFT))
    ce, se, co, so = np.cos(ang(2 * j)), np.sin(ang(2 * j)), np.cos(ang(2 * j + 1)), np.sin(ang(2 * j + 1))
    se[0, :] = (-1.0) ** np.arange(HALF)
    w_e = np.full((HALF, 1), 2.0 / N_FFT)
    w_e[0] = 1.0 / N_FFT
    w_o = 2.0 / N_FFT
    fwd_e = _blocked(ce, so, 0)
    fwd_o = _blocked(co, se, 0)
    inv_s = _blocked((w_e * ce).T, (-w_o * so).T, 1)
    inv_a = _blocked((-w_e * se).T, (w_o * co).T, 1)
    rev = np.eye(HALF)[::-1]
    return tuple(np.ascontiguousarray(t).astype(np.float32) for t in (fwd_e, fwd_o, inv_s, inv_a, rev))


@functools.lru_cache(maxsize=None)
def _filter_dft_table():
    j = np.arange(HALF, dtype=np.int64)[:, None]
    lag = np.arange(SEQ, dtype=np.int64)[None, :]
    ang = lambda k: ((k * lag) % N_FFT).astype(np.float64) * (2.0 * math.pi / N_FFT)
    ce, se, co, so = np.cos(ang(2 * j)), np.sin(ang(2 * j)), np.cos(ang(2 * j + 1)), np.sin(ang(2 * j + 1))
    se[0, :] = (-1.0) ** np.arange(SEQ)
    parts = []
    for b in range(N_FREQ_BLOCKS):
        rows = slice(b * FREQ_BLOCK, (b + 1) * FREQ_BLOCK)
        parts += [ce[rows], se[rows], co[rows], so[rows]]
    return np.concatenate(parts, axis=0).astype(np.float32)


@functools.lru_cache(maxsize=None)
def _filter_tables():
    n = np.arange(SEQ, dtype=np.float64)
    t = n / max(SEQ - 1, 1)
    bands = np.linspace(1e-4, FILTER_BANDS - 1, FILTER_BANDS)
    ang = (2.0 * math.pi / SEQ) * n[:, None] * bands[None, :]
    z = np.concatenate([t[:, None], np.cos(ang), -np.sin(ang)], axis=-1)
    z = np.pad(z, ((0, 0), (0, LANES - FILTER_EMB_DIM)))
    max_decay = math.log(DECAY_TARGET) / FAST_DECAY_PCT
    min_decay = math.log(DECAY_TARGET) / SLOW_DECAY_PCT
    deltas = np.abs(np.linspace(min_decay, max_decay, HYENA_WIDTH))
    window = np.exp(-t[:, None] * deltas[None, :])
    return z.astype(np.float32), window.astype(np.float32)


@functools.lru_cache(maxsize=None)
def _attention_masks():
    i = np.arange(Q_BLOCK)[:, None]
    j = np.arange(K_WINDOW)[None, :]
    classes, ids = {}, []
    for d in DILATIONS:
        n = SEQ // d
        for c in range(SEQ // Q_BLOCK):
            q0 = c * Q_BLOCK
            ws = min(max(q0 - RADIUS, 0), SEQ - K_WINDOW)
            seg = (q0 // n) * n
            kpos = ws + j
            valid = (np.abs(kpos - (q0 + i)) <= RADIUS) & (kpos >= seg) & (kpos < seg + n)
            bias = np.where(valid, 0.0, NEG_INF).astype(np.float32)
            ids.append(classes.setdefault(bias.tobytes(), len(classes)))
    table = np.stack([np.frombuffer(b, np.float32).reshape(Q_BLOCK, K_WINDOW) for b in classes])
    return table, np.asarray(ids, np.int32)


def _rms(x, gain):
    return x * lax.rsqrt(jnp.mean(x * x, axis=-1, keepdims=True) + RMS_EPS) * gain


def _store_row_orders(hn, o0_ref, o1_ref, o2_ref, a_ref, b_ref):
    d1, d2 = DILATIONS[1], DILATIONS[2]
    n1, n2 = hn.shape[0] // d1, hn.shape[0] // d2
    o0_ref[...] = hn.astype(BF16)
    for s in range(N_SLABS):
        a_ref[s] = hn[:, s * LANES:(s + 1) * LANES]
    for r in range(d1):
        v = jnp.concatenate([a_ref[s, pl.ds(r, n1, stride=d1), :] for s in range(N_SLABS)], axis=1)
        o1_ref[r] = v.astype(BF16)
        for s in range(N_SLABS):
            b_ref[s, r * n1:(r + 1) * n1, :] = v[:, s * LANES:(s + 1) * LANES]
    for r in range(d2):
        src = pl.ds((r % d1) * n1 + r // d1, n2, stride=d1)
        v = jnp.concatenate([b_ref[s, src, :] for s in range(N_SLABS)], axis=1)
        o2_ref[r] = v.astype(BF16)


def _row_order_outputs(batch, tile):
    d1, d2 = DILATIONS[1], DILATIONS[2]
    tiles = SEQ // tile
    shapes = [jax.ShapeDtypeStruct((batch * SEQ, D_MODEL), BF16),
              jax.ShapeDtypeStruct((batch, d1, SEQ // d1, D_MODEL), BF16),
              jax.ShapeDtypeStruct((batch, d2, SEQ // d2, D_MODEL), BF16)]
    specs = [pl.BlockSpec((tile, D_MODEL), lambda i: (i, 0)),
             pl.BlockSpec((None, d1, tile // d1, D_MODEL), lambda i: (i // tiles, 0, i % tiles, 0)),
             pl.BlockSpec((None, d2, tile // d2, D_MODEL), lambda i: (i // tiles, 0, i % tiles, 0))]
    scratch = [pltpu.VMEM((N_SLABS, tile, LANES), F32), pltpu.VMEM((N_SLABS, tile, LANES), F32)]
    return shapes, specs, scratch


def _norm_perm_kernel(x_ref, g_ref, o0_ref, o1_ref, o2_ref, a_ref, b_ref):
    _store_row_orders(_rms(x_ref[...], g_ref[...]), o0_ref, o1_ref, o2_ref, a_ref, b_ref)


def _norm_perm(x2d, gain, batch):
    tile = 2 * ROW_TILE
    shapes, specs, scratch = _row_order_outputs(batch, tile)
    return pl.pallas_call(
        _norm_perm_kernel,
        grid=(batch * SEQ // tile,),
        in_specs=[pl.BlockSpec((tile, D_MODEL), lambda i: (i, 0)), pl.BlockSpec((1, D_MODEL), lambda i: (0, 0))],
        out_specs=specs,
        out_shape=shapes,
        scratch_shapes=scratch,
        compiler_params=_params(1),
        name="norm_perm",
    )(x2d, gain.reshape(1, D_MODEL))


def _filter_kernel(z_ref, win_ref, w1_ref, b1_ref, wi_ref, bi_ref, wo_ref, fr_ref, kf_ref, o_ref, hcat_ref):
    i = pl.program_id(0)
    rows_per_chunk = 256
    hi = lax.Precision.HIGHEST
    width = HYENA_WIDTH

    @pl.when(i == 0)
    def _():
        fr = fr_ref[...]

        def chunk(c, carry):
            r0 = pl.multiple_of(c * rows_per_chunk, rows_per_chunk)
            rows = pl.ds(r0, rows_per_chunk)
            hid = jnp.sin(fr * (jnp.dot(z_ref[rows, :], w1_ref[...], precision=hi,
                                        preferred_element_type=F32) + b1_ref[...]))
            for l in range(FILTER_INNER):
                hid = jnp.sin(fr * (jnp.dot(hid, wi_ref[l], precision=hi,
                                            preferred_element_type=F32) + bi_ref[l]))
            filt = jnp.dot(hid.astype(BF16), wo_ref[...].astype(BF16), preferred_element_type=F32)
            win = win_ref[rows, :]
            row = r0 + lax.broadcasted_iota(jnp.int32, (rows_per_chunk, 1), 0)
            h_fwd = filt[:, :width] * win
            h_bwd = jnp.where(row == 0, 0.0, filt[:, width:] * win)
            hcat_ref[rows, :width] = h_fwd.astype(BF16)
            hcat_ref[rows, width:] = h_bwd.astype(BF16)
            return carry

        lax.fori_loop(0, SEQ // rows_per_chunk, chunk, 0)

    def sums(g):
        sp = jnp.dot(kf_ref[g * FREQ_BLOCK:(g + 1) * FREQ_BLOCK, :], hcat_ref[...], preferred_element_type=F32)
        return sp[:, :width], sp[:, width:]

    first_row = (lax.broadcasted_iota(jnp.int32, (FREQ_BLOCK, 1), 0) == 0) & (i == 0)
    f, b = sums(0)
    re_even = f + b
    f, b = sums(1)
    nyquist = f[0:1] + b[0:1]
    o_ref[0] = re_even
    o_ref[1] = jnp.where(first_row, 0.0, b - f)
    o_ref[2] = jnp.where(first_row, nyquist, re_even)
    f, b = sums(2)
    o_ref[3] = f + b
    f, b = sums(3)
    o_ref[4] = b - f


def _hyena_filter_spectrum(z, window, w1, b1, w_inner, b_inner, w_out, freq, filter_dft):
    full = lambda a: pl.BlockSpec(a.shape, lambda i: (0,) * a.ndim)
    w1p = jnp.pad(w1, ((0, LANES - FILTER_EMB_DIM), (0, 0)))
    b1 = b1.reshape(1, FILTER_HIDDEN)
    b_inner = b_inner.reshape(FILTER_INNER, 1, FILTER_HIDDEN)
    freq = freq.reshape(1, FILTER_HIDDEN)
    ops = (z, window, w1p, b1, w_inner, b_inner, w_out, freq)
    return pl.pallas_call(
        _filter_kernel,
        grid=(N_FREQ_BLOCKS,),
        in_specs=[full(a) for a in ops] + [pl.BlockSpec((4 * FREQ_BLOCK, SEQ), lambda i: (i, 0))],
        out_specs=pl.BlockSpec((None, N_FILTER_SETS, FREQ_BLOCK, HYENA_WIDTH), lambda i: (i, 0, 0, 0)),
        out_shape=jax.ShapeDtypeStruct((N_FREQ_BLOCKS, N_FILTER_SETS, FREQ_BLOCK, HYENA_WIDTH), F32),
        scratch_shapes=[pltpu.VMEM((SEQ, 2 * HYENA_WIDTH), BF16)],
        compiler_params=_params(1),
        name="hyena_filter",
    )(*ops, filter_dft)


def _hy_proj_kernel(hn_ref, w_ref, cw_ref, cb_ref, x0_ref, uv_ref, wb_ref):
    _cast_weights_once(w_ref, wb_ref)
    n_chunks = SEQ // ROW_CHUNK
    pre = [jnp.dot(hn_ref[rc * ROW_CHUNK:(rc + 1) * ROW_CHUNK, :], wb_ref[...], preferred_element_type=F32)
           for rc in range(n_chunks)]
    w = cw_ref[...]
    bias = cb_ref[...]
    zeros = jnp.zeros((SUBLANES, HY_IN_WIDTH), F32)
    for rc in range(n_chunks):
        before = pre[rc - 1][ROW_CHUNK - SUBLANES:] if rc > 0 else zeros
        after = pre[rc + 1][:SUBLANES] if rc < n_chunks - 1 else zeros
        ext = ROW_CHUNK + SUBLANES
        prev = pltpu.roll(jnp.concatenate([before, pre[rc]], axis=0), 1, 0)[SUBLANES:]
        nxt = pltpu.roll(jnp.concatenate([pre[rc], after], axis=0), ext - 1, 0)[:ROW_CHUNK]
        u = bias + prev * w[0:1] + pre[rc] * w[1:2] + nxt * w[2:3]
        rows = slice(rc * ROW_CHUNK, (rc + 1) * ROW_CHUNK)
        x0_ref[rows, :] = u[:, :HYENA_WIDTH].astype(BF16)
        uv_ref[rows, :] = (u[:, HYENA_WIDTH:2 * HYENA_WIDTH] * u[:, 2 * HYENA_WIDTH:]).astype(BF16)


def _hy_proj(hn0, w_in, layer, conv_w, conv_b, batch):
    const = lambda a: pl.BlockSpec(a.shape, lambda b: (0,) * a.ndim)
    conv_b = conv_b.reshape(1, HY_IN_WIDTH)
    out_spec = pl.BlockSpec((SEQ, HYENA_WIDTH), lambda b: (b, 0))
    out_shape = jax.ShapeDtypeStruct((batch * SEQ, HYENA_WIDTH), BF16)
    return pl.pallas_call(
        _hy_proj_kernel,
        grid=(batch,),
        in_specs=[pl.BlockSpec((SEQ, D_MODEL), lambda b: (b, 0)), _weight_block(layer, HY_IN_WIDTH, 0),
                  const(conv_w), const(conv_b)],
        out_specs=[out_spec, out_spec],
        out_shape=[out_shape, out_shape],
        scratch_shapes=[pltpu.VMEM((D_MODEL, HY_IN_WIDTH), BF16)],
        compiler_params=_params(1),
        name="hy_proj",
    )(hn0, w_in, conv_w, conv_b)


def _cast_weights_once(w_ref, wb_ref):
    @pl.when(pl.program_id(0) == 0)
    def _():
        wb_ref[...] = w_ref[...].astype(BF16)


def _weight_block(layer, width, block):
    return pl.BlockSpec((None, D_MODEL, width), lambda b: (layer, 0, block), pipeline_mode=pl.Buffered(1))


def _qkv_proj_kernel(hn_ref, w_ref, rope_ref, o_ref, wb_ref):
    _cast_weights_once(w_ref, wb_ref)
    half = HEAD_DIM // 2
    q_scale = HEAD_DIM ** -0.5 * math.log2(math.e)
    for rc in range(SEQ // ROW_CHUNK):
        rows = slice(rc * ROW_CHUNK, (rc + 1) * ROW_CHUNK)
        acc = jnp.dot(hn_ref[rows, :], wb_ref[...], preferred_element_type=F32)
        cos, sa, sb = rope_ref[0, rows, :], rope_ref[1, rows, :], rope_ref[2, rows, :]
        for c in range(2 * ATTN_WIDTH // LANES):
            t = acc[:, c * LANES:(c + 1) * LANES]
            if c < ATTN_WIDTH // LANES:
                t = t * q_scale
            out = t * cos + pltpu.roll(t, LANES - half, 1) * sa + pltpu.roll(t, half, 1) * sb
            o_ref[rows, c * LANES:(c + 1) * LANES] = out.astype(BF16)
        o_ref[rows, 2 * ATTN_WIDTH:] = acc[:, 2 * ATTN_WIDTH:].astype(BF16)


def _qkv_proj(hn, w_in, layer, group, rope, batch):
    const = lambda a: pl.BlockSpec(a.shape, lambda b: (0,) * a.ndim)
    assert HY_IN_WIDTH == GROUP_WIDTH
    return pl.pallas_call(
        _qkv_proj_kernel,
        grid=(batch,),
        in_specs=[pl.BlockSpec((SEQ, D_MODEL), lambda b: (b, 0)), _weight_block(layer, GROUP_WIDTH, 1 + group),
                  const(rope)],
        out_specs=pl.BlockSpec((SEQ, GROUP_WIDTH), lambda b: (b, 0)),
        out_shape=jax.ShapeDtypeStruct((batch * SEQ, GROUP_WIDTH), BF16),
        scratch_shapes=[pltpu.VMEM((D_MODEL, GROUP_WIDTH), BF16)],
        compiler_params=_params(1),
        name="qkv_proj",
    )(hn, w_in, rope)


def _hyena_kernel(x0_ref, uv_ref, skip_ref, rev_ref, fe_ref, fo_ref, gs_ref, ga_ref, h_ref, o_ref):
    lo = uv_ref[:HALF, :].astype(F32)
    mirrored = jnp.dot(rev_ref[...], uv_ref[HALF:, :], preferred_element_type=F32)
    e = (lo + mirrored).astype(BF16)
    d = (lo - mirrored).astype(BF16)
    sym = anti = None
    for blk in range(N_FREQ_BLOCKS):
        span = slice(blk * 2 * FREQ_BLOCK, (blk + 1) * 2 * FREQ_BLOCK)
        from_e = jnp.dot(fe_ref[span, :], e, preferred_element_type=F32)
        from_d = jnp.dot(fo_ref[span, :], d, preferred_element_type=F32)
        a_e, b_o = from_e[:FREQ_BLOCK], from_e[FREQ_BLOCK:]
        a_o, b_e = from_d[:FREQ_BLOCK], from_d[FREQ_BLOCK:]
        yr_e = a_e * h_ref[blk, 0] + b_e * h_ref[blk, 1]
        yi_e = a_e * h_ref[blk, 1] - b_e * h_ref[blk, 2]
        yr_o = a_o * h_ref[blk, 3] + b_o * h_ref[blk, 4]
        yi_o = a_o * h_ref[blk, 4] - b_o * h_ref[blk, 3]
        part_s = jnp.dot(gs_ref[:, span], jnp.concatenate([yr_e, yi_o], axis=0).astype(BF16),
                         preferred_element_type=F32)
        part_a = jnp.dot(ga_ref[:, span], jnp.concatenate([yi_e, yr_o], axis=0).astype(BF16),
                         preferred_element_type=F32)
        sym = part_s if sym is None else sym + part_s
        anti = part_a if anti is None else anti + part_a
    skip = skip_ref[...]
    y_lo = sym + anti + skip * lo
    y_hi = (jnp.dot(rev_ref[...], (sym - anti).astype(BF16), preferred_element_type=F32)
            + skip * uv_ref[HALF:, :].astype(F32))
    o_ref[:HALF, :] = (x0_ref[:HALF, :].astype(F32) * y_lo).astype(BF16)
    o_ref[HALF:, :] = (x0_ref[HALF:, :].astype(F32) * y_hi).astype(BF16)


def _hyena(x0c, uv, skip, fold_dft, filters, batch):
    slab = pl.BlockSpec((SEQ, HYENA_WIDTH), lambda b: (b, 0))
    resident = lambda a: pl.BlockSpec(a.shape, lambda b: (0,) * a.ndim, pipeline_mode=pl.Buffered(1))
    fwd_e, fwd_o, inv_s, inv_a, rev = fold_dft
    tables = (rev, fwd_e, fwd_o, inv_s, inv_a, filters)
    return pl.pallas_call(
        _hyena_kernel,
        grid=(batch,),
        in_specs=[slab, slab, pl.BlockSpec((1, HYENA_WIDTH), lambda b: (0, 0))] + [resident(t) for t in tables],
        out_specs=slab,
        out_shape=jax.ShapeDtypeStruct((batch * SEQ, HYENA_WIDTH), BF16),
        compiler_params=_params(1),
        name="hyena",
    )(x0c, uv, skip.reshape(1, HYENA_WIDTH), *tables)


def _attention_kernel(cls_ref, bias_ref, *refs):
    qkv_refs = refs[:3 * N_GROUPS]
    o_ref, acc_ref, den_ref, max_ref = refs[3 * N_GROUPS:]
    lane = lax.broadcasted_iota(jnp.int32, (1, LANES), 1)
    first = lane < HEAD_DIM
    ones = jnp.ones((K_WINDOW, LANES), BF16)
    n_blocks = SEQ // Q_BLOCK

    for g, d in reversed(list(enumerate(DILATIONS))):
        q_ref, k_ref, v_ref = qkv_refs[3 * g:3 * g + 3]
        n = SEQ // d

        def q_block(c, carry, g=g, d=d, n=n, q_ref=q_ref, k_ref=k_ref, v_ref=v_ref):
            q0 = pl.multiple_of(c * Q_BLOCK, Q_BLOCK)
            ws = pl.multiple_of(jnp.clip(q0 - RADIUS, 0, SEQ - K_WINDOW), RADIUS)
            q = q_ref[pl.ds(q0, Q_BLOCK), :]
            zero = jnp.zeros_like(q)
            q2 = jnp.concatenate([jnp.where(first, q, zero), jnp.where(first, zero, q)], axis=0)
            s = lax.dot_general(q2, k_ref[pl.ds(ws, K_WINDOW), :], (((1,), (1,)), ((), ())),
                                preferred_element_type=F32)
            bias = bias_ref[cls_ref[g * n_blocks + c]]
            s = s + jnp.concatenate([bias, bias], axis=0)
            m = jnp.max(s, axis=1, keepdims=True)
            p = jnp.exp2(s - m).astype(BF16)
            out = jnp.dot(p, jnp.concatenate([v_ref[pl.ds(ws, K_WINDOW), :], ones], axis=1),
                          preferred_element_type=F32)
            acc = jnp.where(first, out[:Q_BLOCK, :LANES], out[Q_BLOCK:, :LANES])
            den = jnp.where(first, out[:Q_BLOCK, LANES:], out[Q_BLOCK:, LANES:])
            top = jnp.where(first, m[:Q_BLOCK], m[Q_BLOCK:]) + jnp.zeros((Q_BLOCK, LANES), F32)
            if d > 1:
                dst = pl.ds(q0 // n + d * lax.rem(q0, n), Q_BLOCK, stride=d)
                acc_ref[g - 1, dst, :] = acc
                den_ref[g - 1, dst, :] = den
                max_ref[g - 1, dst, :] = top
            else:
                rows = pl.ds(q0, Q_BLOCK)
                ms = [top] + [max_ref[o, rows, :] for o in range(N_GROUPS - 1)]
                accs = [acc] + [acc_ref[o, rows, :] for o in range(N_GROUPS - 1)]
                dens = [den] + [den_ref[o, rows, :] for o in range(N_GROUPS - 1)]
                peak = functools.reduce(jnp.maximum, ms)
                wts = [jnp.exp2(mm - peak) for mm in ms]
                num = sum(w * a for w, a in zip(wts, accs))
                tot = sum(w * dd for w, dd in zip(wts, dens))
                o_ref[rows, :] = (num / tot).astype(BF16)
            return carry

        lax.fori_loop(0, n_blocks, q_block, 0, unroll=n_blocks)


def _attention(qkvs, batch):
    pairs = ATTN_WIDTH // LANES
    table, ids = (jnp.asarray(a) for a in _attention_masks())
    specs = [pl.BlockSpec(memory_space=pltpu.SMEM),
             pl.BlockSpec(table.shape, lambda b, hp: (0, 0, 0))]
    operands = []
    for g in range(N_GROUPS):
        for t in range(3):
            specs.append(pl.BlockSpec((SEQ, LANES), functools.partial(lambda b, hp, t: (b, t * pairs + hp), t=t)))
            operands.append(qkvs[g])
    group_rows = pltpu.VMEM((N_GROUPS - 1, SEQ, LANES), F32)
    return pl.pallas_call(
        _attention_kernel,
        grid=(batch, pairs),
        in_specs=specs,
        out_specs=pl.BlockSpec((SEQ, LANES), lambda b, hp: (b, hp)),
        out_shape=jax.ShapeDtypeStruct((batch * SEQ, ATTN_WIDTH), BF16),
        scratch_shapes=[group_rows, group_rows, group_rows],
        compiler_params=_params(2),
        name="attention",
    )(ids, table, *operands)


def _merge_kernel(x_ref, hn_ref, yh_ref, ya_ref, wg_ref, ph_ref, pa_ref, wo_ref, o_ref):
    hn = hn_ref[...]
    g_hy = jax.nn.sigmoid(jnp.dot(hn, wg_ref[:, :D_MODEL], preferred_element_type=F32))
    merged = g_hy * jnp.dot(yh_ref[...], ph_ref[...], preferred_element_type=F32)
    g_att = jax.nn.sigmoid(jnp.dot(hn, wg_ref[:, D_MODEL:], preferred_element_type=F32))
    merged = merged + g_att * jnp.dot(ya_ref[...], pa_ref[...], preferred_element_type=F32)
    o_ref[...] = x_ref[...] + jnp.dot(merged.astype(BF16), wo_ref[...], preferred_element_type=F32)


def _merge(x2d, hn0, y_hy, y_att, w_gate, p_hy, p_att, w_o, rows):
    tm = 1024
    row = lambda w: pl.BlockSpec((tm, w), lambda i: (i, 0))
    const = lambda a: pl.BlockSpec(a.shape, lambda i: (0, 0))
    return pl.pallas_call(
        _merge_kernel,
        grid=(rows // tm,),
        in_specs=[row(D_MODEL), row(D_MODEL), row(HYENA_WIDTH), row(ATTN_WIDTH),
                  const(w_gate), const(p_hy), const(p_att), const(w_o)],
        out_specs=row(D_MODEL),
        out_shape=jax.ShapeDtypeStruct((rows, D_MODEL), F32),
        compiler_params=_params(1),
        name="merge",
    )(x2d, hn0, y_hy, y_att, w_gate, p_hy, p_att, w_o)


def _ffn_kernel(x_ref, g_ref, w1_ref, w2_ref, gn_ref, *rest, last):
    x = x_ref[...]
    hn = _rms(x, g_ref[...]).astype(BF16)
    chunk = 1024
    y = x
    for c in range(D_FF // chunk):
        h = jnp.dot(hn, w1_ref[:, c * chunk:(c + 1) * chunk], preferred_element_type=F32)
        h = jnp.square(jnp.maximum(h, 0.0)).astype(BF16)
        y = y + jnp.dot(h, w2_ref[c * chunk:(c + 1) * chunk, :], preferred_element_type=F32)
    if last:
        (o_ref,) = rest
        o_ref[...] = _rms(y, gn_ref[...])
    else:
        o_ref, o0_ref, o1_ref, o2_ref, a_ref, b_ref = rest
        o_ref[...] = y
        _store_row_orders(_rms(y, gn_ref[...]), o0_ref, o1_ref, o2_ref, a_ref, b_ref)


def _ffn(x2d, gain, w1, w2, gain_next, batch, last):
    rows = batch * SEQ
    tile = 2 * ROW_TILE if last else ROW_TILE
    row = pl.BlockSpec((tile, D_MODEL), lambda i: (i, 0))
    const = lambda a: pl.BlockSpec(a.shape, lambda i: (0, 0))
    gain = gain.reshape(1, D_MODEL)
    gain_next = gain_next.reshape(1, D_MODEL)
    out_shape = [jax.ShapeDtypeStruct((rows, D_MODEL), F32)]
    out_specs = [row]
    scratch = []
    if not last:
        shapes, specs, scratch = _row_order_outputs(batch, tile)
        out_shape += shapes
        out_specs += specs
    return pl.pallas_call(
        functools.partial(_ffn_kernel, last=last),
        grid=(rows // tile,),
        in_specs=[row, const(gain), const(w1), const(w2), const(gain_next)],
        out_specs=out_specs,
        out_shape=out_shape,
        scratch_shapes=scratch,
        compiler_params=_params(1),
        name="ffn",
    )(x2d, gain, w1, w2, gain_next)


def kernel(x, norm_mix, w_in, conv_w, conv_b, filt_w1, filt_b1, filt_w_inner, filt_b_inner, filt_w_out,
           filt_freq, hy_skip, p_hy, p_att, w_o, norm_ffn, w_ff1, w_ff2, norm_final):
    batch, seq, d_model = x.shape
    assert (seq, d_model) == (SEQ, D_MODEL)
    depth = norm_mix.shape[0]
    rows = batch * seq
    bf = lambda a: a.astype(BF16)
    rope = tuple(jnp.asarray(t) for t in _rope_tables())
    fold_dft = tuple(bf(jnp.asarray(t)) for t in _fold_dft_tables())
    filter_dft = bf(jnp.asarray(_filter_dft_table()))
    z, window = (jnp.asarray(t) for t in _filter_tables())

    x2d = x.reshape(rows, d_model)
    hns = _norm_perm(x2d, norm_mix[0], batch)
    for l in range(depth):
        hns = [h.reshape(rows, d_model) for h in hns]
        filters = _hyena_filter_spectrum(z, window, filt_w1[l], filt_b1[l], filt_w_inner[l], filt_b_inner[l],
                                         filt_w_out[l], filt_freq[l], filter_dft)
        x0c, uv = _hy_proj(hns[0], w_in, l, conv_w[l], conv_b[l], batch)
        y_hy = _hyena(x0c, uv, hy_skip[l], fold_dft, filters, batch)
        qkvs = [_qkv_proj(hns[g], w_in, l, g, rope[g], batch) for g in range(N_GROUPS)]
        y_att = _attention(qkvs, batch)
        x2d = _merge(x2d, hns[0], y_hy, y_att, bf(w_in[l, :, HY_IN_WIDTH + QKV_WIDTH:]), bf(p_hy[l]), bf(p_att[l]),
                     bf(w_o[l]), rows)
        last = l == depth - 1
        outs = _ffn(x2d, norm_ffn[l], bf(w_ff1[l]), bf(w_ff2[l]), norm_final if last else norm_mix[l + 1], batch, last)
        x2d, hns = outs[0], outs[1:]
    return x2d.reshape(batch, seq, d_model)
```

```python
import functools
import math

import numpy as np
import jax
import jax.numpy as jnp
from jax import lax
from jax.experimental import pallas as pl
from jax.experimental.pallas import tpu as pltpu

F32 = jnp.float32
BF16 = jnp.bfloat16

D_MODEL = 1024
SEQ = 2048
HEAD_DIM = 64
ATTN_HEADS = 8
DILATIONS = (1, 4, 16)
RADIUS = 64
N_GROUPS = len(DILATIONS)
ATTN_WIDTH = ATTN_HEADS * HEAD_DIM
ROPE_THETA = 10000.0
NEG_INF = -1e30
HYENA_WIDTH = D_MODEL // 2
FILTER_BANDS = 16
FILTER_EMB_DIM = 1 + 2 * FILTER_BANDS
FILTER_HIDDEN = 64
FILTER_INNER = 2
DECAY_TARGET = 1e-2
FAST_DECAY_PCT = 0.3
SLOW_DECAY_PCT = 1.5
HY_IN_WIDTH = 3 * HYENA_WIDTH
GROUP_WIDTH = 3 * ATTN_WIDTH
QKV_WIDTH = N_GROUPS * GROUP_WIDTH
D_FF = 4 * D_MODEL
RMS_EPS = 1e-6

LANES = 128
SUBLANES = 8
N_SLABS = D_MODEL // LANES
N_FFT = 2 * SEQ
HALF = SEQ // 2
FREQ_BLOCK = 512
N_FREQ_BLOCKS = HALF // FREQ_BLOCK
N_FILTER_SETS = 5
FILTER_CHUNK = 64
MLP_ROWS = 128
PHASE_STEPS = SEQ // MLP_ROWS
assert PHASE_STEPS == HALF // FILTER_CHUNK and FREQ_BLOCK % FILTER_CHUNK == 0
Q_BLOCK = 128
K_WINDOW = 256
ROW_TILE = 512
ROW_CHUNK = 512
VMEM_LIMIT = 56 * 1024 * 1024

assert DILATIONS[0] == 1 and DILATIONS[2] == DILATIONS[1] ** 2


def _params(n_axes, vmem=VMEM_LIMIT):
    return pltpu.CompilerParams(dimension_semantics=("arbitrary",) * n_axes, vmem_limit_bytes=vmem)


def _perm_positions(dilation):
    n = SEQ // dilation
    r = np.arange(SEQ) // n
    m = np.arange(SEQ) % n
    return m * dilation + r


@functools.lru_cache(maxsize=None)
def _rope_tables():
    half = HEAD_DIM // 2
    inv_freq = ROPE_THETA ** (-np.arange(half, dtype=np.float64) / half)
    lane = np.arange(LANES)
    first_half = (lane % HEAD_DIM) < half
    tables = []
    for d in DILATIONS:
        ang = _perm_positions(d)[:, None].astype(np.float64) * inv_freq[None, :]
        cos_l = np.cos(ang)[:, lane % half]
        sin_l = np.sin(ang)[:, lane % half]
        sa = np.where(first_half[None, :], -sin_l, 0.0)
        sb = np.where(first_half[None, :], 0.0, sin_l)
        tables.append(np.stack([cos_l, sa, sb]).astype(np.float32))
    return tuple(tables)


def _blocked(rows_a, rows_b, axis):
    parts = []
    for b in range(N_FREQ_BLOCKS):
        sl = [slice(None)] * rows_a.ndim
        sl[axis] = slice(b * FREQ_BLOCK, (b + 1) * FREQ_BLOCK)
        parts += [rows_a[tuple(sl)], rows_b[tuple(sl)]]
    return np.concatenate(parts, axis=axis)


@functools.lru_cache(maxsize=None)
def _fold_dft_tables():
    j = np.arange(HALF, dtype=np.int64)[:, None]
    n = np.arange(HALF, dtype=np.int64)[None, :]
    ang = lambda k: ((k * (2 * n + 1)) % (2 * N_FFT)).astype(np.float64) * (2.0 * math.pi / (2 * N_FFT))
    ce, se, co, so = np.cos(ang(2 * j)), np.sin(ang(2 * j)), np.cos(ang(2 * j + 1)), np.sin(ang(2 * j + 1))
    se[0, :] = (-1.0) ** np.arange(HALF)
    w_e = np.full((HALF, 1), 2.0 / N_FFT)
    w_e[0] = 1.0 / N_FFT
    w_o = 2.0 / N_FFT
    fwd_e = _blocked(ce, so, 0)
    fwd_o = _blocked(co, se, 0)
    inv_s = _blocked((w_e * ce).T, (-w_o * so).T, 1)
    inv_a = _blocked((-w_e * se).T, (w_o * co).T, 1)
    rev = np.eye(HALF)[::-1]
    return tuple(np.ascontiguousarray(t).astype(np.float32) for t in (fwd_e, fwd_o, inv_s, inv_a, rev))


@functools.lru_cache(maxsize=None)
def _filter_dft_table():
    j = np.arange(HALF, dtype=np.int64)[:, None]
    lag = np.arange(SEQ, dtype=np.int64)[None, :]
    ang = lambda k: ((k * lag) % N_FFT).astype(np.float64) * (2.0 * math.pi / N_FFT)
    ce, se, co, so = np.cos(ang(2 * j)), np.sin(ang(2 * j)), np.cos(ang(2 * j + 1)), np.sin(ang(2 * j + 1))
    se[0, :] = (-1.0) ** np.arange(SEQ)
    parts = []
    for c in range(HALF // FILTER_CHUNK):
        rows = slice(c * FILTER_CHUNK, (c + 1) * FILTER_CHUNK)
        parts += [ce[rows], se[rows], co[rows], so[rows]]
    return np.concatenate(parts, axis=0).astype(np.float32)


@functools.lru_cache(maxsize=None)
def _filter_tables():
    n = np.arange(SEQ, dtype=np.float64)
    t = n / max(SEQ - 1, 1)
    bands = np.linspace(1e-4, FILTER_BANDS - 1, FILTER_BANDS)
    ang = (2.0 * math.pi / SEQ) * n[:, None] * bands[None, :]
    z = np.concatenate([t[:, None], np.cos(ang), -np.sin(ang)], axis=-1)
    z = np.pad(z, ((0, 0), (0, LANES - FILTER_EMB_DIM)))
    max_decay = math.log(DECAY_TARGET) / FAST_DECAY_PCT
    min_decay = math.log(DECAY_TARGET) / SLOW_DECAY_PCT
    deltas = np.abs(np.linspace(min_decay, max_decay, HYENA_WIDTH))
    window = np.exp(-t[:, None] * deltas[None, :])
    return z.astype(np.float32), window.astype(np.float32)


@functools.lru_cache(maxsize=None)
def _attention_masks():
    i = np.arange(Q_BLOCK)[:, None]
    j = np.arange(K_WINDOW)[None, :]
    classes, ids = {}, []
    for d in DILATIONS:
        n = SEQ // d
        for c in range(SEQ // Q_BLOCK):
            q0 = c * Q_BLOCK
            ws = min(max(q0 - RADIUS, 0), SEQ - K_WINDOW)
            seg = (q0 // n) * n
            kpos = ws + j
            valid = (np.abs(kpos - (q0 + i)) <= RADIUS) & (kpos >= seg) & (kpos < seg + n)
            bias = np.where(valid, 0.0, NEG_INF).astype(np.float32)
            ids.append(classes.setdefault(bias.tobytes(), len(classes)))
    table = np.stack([np.frombuffer(b, np.float32).reshape(Q_BLOCK, K_WINDOW) for b in classes])
    return table, np.asarray(ids, np.int32)


def _rms(x, gain):
    return x * lax.rsqrt(jnp.mean(x * x, axis=-1, keepdims=True) + RMS_EPS) * gain


def _store_row_orders(hn, o0_ref, o1_ref, o2_ref, a_ref, b_ref):
    d1, d2 = DILATIONS[1], DILATIONS[2]
    n1, n2 = hn.shape[0] // d1, hn.shape[0] // d2
    o0_ref[...] = hn.astype(BF16)
    for s in range(N_SLABS):
        a_ref[s] = hn[:, s * LANES:(s + 1) * LANES]
    for r in range(d1):
        v = jnp.concatenate([a_ref[s, pl.ds(r, n1, stride=d1), :] for s in range(N_SLABS)], axis=1)
        o1_ref[r] = v.astype(BF16)
        for s in range(N_SLABS):
            b_ref[s, r * n1:(r + 1) * n1, :] = v[:, s * LANES:(s + 1) * LANES]
    for r in range(d2):
        src = pl.ds((r % d1) * n1 + r // d1, n2, stride=d1)
        v = jnp.concatenate([b_ref[s, src, :] for s in range(N_SLABS)], axis=1)
        o2_ref[r] = v.astype(BF16)


def _row_order_outputs(batch, tile):
    d1, d2 = DILATIONS[1], DILATIONS[2]
    tiles = SEQ // tile
    shapes = [jax.ShapeDtypeStruct((batch * SEQ, D_MODEL), BF16),
              jax.ShapeDtypeStruct((batch, d1, SEQ // d1, D_MODEL), BF16),
              jax.ShapeDtypeStruct((batch, d2, SEQ // d2, D_MODEL), BF16)]
    specs = [pl.BlockSpec((tile, D_MODEL), lambda i: (i, 0)),
             pl.BlockSpec((None, d1, tile // d1, D_MODEL), lambda i: (i // tiles, 0, i % tiles, 0)),
             pl.BlockSpec((None, d2, tile // d2, D_MODEL), lambda i: (i // tiles, 0, i % tiles, 0))]
    scratch = [pltpu.VMEM((N_SLABS, tile, LANES), F32), pltpu.VMEM((N_SLABS, tile, LANES), F32)]
    return shapes, specs, scratch


def _filter_step(i, depth):
    phase = i // PHASE_STEPS
    sub = i % PHASE_STEPS
    active = phase < 2 * depth
    layer = jnp.minimum(phase // 2, depth - 1)
    is_mlp = active & (phase % 2 == 0)
    is_spec = active & (phase % 2 == 1)
    chunk = jnp.where(is_spec, sub, jnp.where(active, 0, PHASE_STEPS - 1))
    return layer, is_mlp, is_spec, sub, chunk


def _prologue_kernel(x_ref, g_ref, z_ref, win_ref, w1_ref, b1_ref, wi_ref, bi_ref, wo_ref, fr_ref, kf_ref,
                     o0_ref, o1_ref, o2_ref, h_ref, a_ref, b_ref, hcat_ref, *, depth, n_tiles):
    i = pl.program_id(0)
    hi = lax.Precision.HIGHEST
    width = HYENA_WIDTH

    @pl.when(i < n_tiles)
    def _():
        _store_row_orders(_rms(x_ref[...], g_ref[...]), o0_ref, o1_ref, o2_ref, a_ref, b_ref)

    layer, is_mlp, is_spec, sub, _ = _filter_step(i, depth)

    @pl.when(is_mlp)
    def _():
        r0 = pl.multiple_of(sub * MLP_ROWS, MLP_ROWS)
        rows = pl.ds(r0, MLP_ROWS)
        fr = fr_ref[layer]
        hid = jnp.sin(fr * (jnp.dot(z_ref[rows, :], w1_ref[layer], precision=hi,
                                    preferred_element_type=F32) + b1_ref[layer]))
        for l in range(FILTER_INNER):
            hid = jnp.sin(fr * (jnp.dot(hid, wi_ref[layer, l], precision=hi,
                                        preferred_element_type=F32) + bi_ref[layer, l]))
        filt = jnp.dot(hid.astype(BF16), wo_ref[layer].astype(BF16), preferred_element_type=F32)
        win = win_ref[rows, :]
        row = r0 + lax.broadcasted_iota(jnp.int32, (MLP_ROWS, 1), 0)
        h_fwd = filt[:, :width] * win
        h_bwd = jnp.where(row == 0, 0.0, filt[:, width:] * win)
        hcat_ref[layer, rows, :width] = h_fwd.astype(BF16)
        hcat_ref[layer, rows, width:] = h_bwd.astype(BF16)

    @pl.when(is_spec)
    def _():
        sp = jnp.dot(kf_ref[...], hcat_ref[layer], preferred_element_type=F32)
        f, b = sp[:, :width], sp[:, width:]
        sets = [slice(g * FILTER_CHUNK, (g + 1) * FILTER_CHUNK) for g in range(4)]
        first_row = (lax.broadcasted_iota(jnp.int32, (FILTER_CHUNK, 1), 0) == 0) & (sub == 0)
        re_even = f[sets[0]] + b[sets[0]]
        nyquist = f[sets[1]][0:1] + b[sets[1]][0:1]
        h_ref[0] = re_even
        h_ref[1] = jnp.where(first_row, 0.0, b[sets[1]] - f[sets[1]])
        h_ref[2] = jnp.where(first_row, nyquist, re_even)
        h_ref[3] = f[sets[2]] + b[sets[2]]
        h_ref[4] = b[sets[3]] - f[sets[3]]


def _prologue(x2d, gain, z, window, w1, b1, w_inner, b_inner, w_out, freq, filter_dft, batch):
    depth = w1.shape[0]
    n_tiles = batch * SEQ // ROW_TILE
    steps = max(n_tiles, 2 * depth * PHASE_STEPS)
    resident = lambda a: pl.BlockSpec(a.shape, lambda i: (0,) * a.ndim, pipeline_mode=pl.Buffered(1))
    w1p = jnp.pad(w1, ((0, 0), (0, LANES - FILTER_EMB_DIM), (0, 0)))
    b1 = b1.reshape(depth, 1, FILTER_HIDDEN)
    b_inner = b_inner.reshape(depth, FILTER_INNER, 1, FILTER_HIDDEN)
    freq = freq.reshape(depth, 1, FILTER_HIDDEN)
    consts = (z, window, w1p, b1, w_inner, b_inner, w_out, freq)
    shapes, specs, scratch = _row_order_outputs(batch, ROW_TILE)
    tile_of = lambda i: jnp.minimum(i, n_tiles - 1)
    specs = [pl.BlockSpec(s.block_shape, functools.partial(lambda i, m: m(tile_of(i)), m=s.index_map)) for s in specs]
    per_block = FREQ_BLOCK // FILTER_CHUNK

    def filter_window(i):
        layer, _, _, _, chunk = _filter_step(i, depth)
        return layer, chunk // per_block, 0, chunk % per_block, 0

    outs = pl.pallas_call(
        functools.partial(_prologue_kernel, depth=depth, n_tiles=n_tiles),
        grid=(steps,),
        in_specs=[pl.BlockSpec((ROW_TILE, D_MODEL), lambda i: (tile_of(i), 0)),
                  pl.BlockSpec((1, D_MODEL), lambda i: (0, 0))] + [resident(a) for a in consts]
                 + [pl.BlockSpec((4 * FILTER_CHUNK, SEQ), lambda i: (_filter_step(i, depth)[4], 0))],
        out_specs=specs + [pl.BlockSpec((None, None, N_FILTER_SETS, FILTER_CHUNK, HYENA_WIDTH), filter_window)],
        out_shape=shapes + [jax.ShapeDtypeStruct((depth, N_FREQ_BLOCKS, N_FILTER_SETS, FREQ_BLOCK, HYENA_WIDTH), F32)],
        scratch_shapes=scratch + [pltpu.VMEM((depth, SEQ, 2 * HYENA_WIDTH), BF16)],
        compiler_params=_params(1),
        name="prologue",
    )(x2d, gain.reshape(1, D_MODEL), *consts, filter_dft)
    return outs[:3], outs[3]


def _hy_proj_kernel(hn_ref, w_ref, cw_ref, cb_ref, x0_ref, uv_ref, wb_ref):
    _cast_weights_once(w_ref, wb_ref)
    n_chunks = SEQ // ROW_CHUNK
    pre = [jnp.dot(hn_ref[rc * ROW_CHUNK:(rc + 1) * ROW_CHUNK, :], wb_ref[...], preferred_element_type=F32)
           for rc in range(n_chunks)]
    w = cw_ref[...]
    bias = cb_ref[...]
    zeros = jnp.zeros((SUBLANES, HY_IN_WIDTH), F32)
    for rc in range(n_chunks):
        before = pre[rc - 1][ROW_CHUNK - SUBLANES:] if rc > 0 else zeros
        after = pre[rc + 1][:SUBLANES] if rc < n_chunks - 1 else zeros
        ext = ROW_CHUNK + SUBLANES
        prev = pltpu.roll(jnp.concatenate([before, pre[rc]], axis=0), 1, 0)[SUBLANES:]
        nxt = pltpu.roll(jnp.concatenate([pre[rc], after], axis=0), ext - 1, 0)[:ROW_CHUNK]
        u = bias + prev * w[0:1] + pre[rc] * w[1:2] + nxt * w[2:3]
        rows = slice(rc * ROW_CHUNK, (rc + 1) * ROW_CHUNK)
        x0_ref[rows, :] = u[:, :HYENA_WIDTH].astype(BF16)
        uv_ref[rows, :] = (u[:, HYENA_WIDTH:2 * HYENA_WIDTH] * u[:, 2 * HYENA_WIDTH:]).astype(BF16)


def _hy_proj(hn0, w_in, layer, conv_w, conv_b, batch):
    const = lambda a: pl.BlockSpec(a.shape, lambda b: (0,) * a.ndim)
    conv_b = conv_b.reshape(1, HY_IN_WIDTH)
    out_spec = pl.BlockSpec((SEQ, HYENA_WIDTH), lambda b: (b, 0))
    out_shape = jax.ShapeDtypeStruct((batch * SEQ, HYENA_WIDTH), BF16)
    return pl.pallas_call(
        _hy_proj_kernel,
        grid=(batch,),
        in_specs=[pl.BlockSpec((SEQ, D_MODEL), lambda b: (b, 0)), _weight_block(layer, HY_IN_WIDTH, 0),
                  const(conv_w), const(conv_b)],
        out_specs=[out_spec, out_spec],
        out_shape=[out_shape, out_shape],
        scratch_shapes=[pltpu.VMEM((D_MODEL, HY_IN_WIDTH), BF16)],
        compiler_params=_params(1),
        name="hy_proj",
    )(hn0, w_in, conv_w, conv_b)


def _cast_weights_once(w_ref, wb_ref):
    @pl.when(pl.program_id(0) == 0)
    def _():
        wb_ref[...] = w_ref[...].astype(BF16)


def _weight_block(layer, width, block):
    return pl.BlockSpec((None, D_MODEL, width), lambda b: (layer, 0, block), pipeline_mode=pl.Buffered(1))


def _qkv_proj_kernel(hn_ref, w_ref, rope_ref, o_ref, wb_ref):
    _cast_weights_once(w_ref, wb_ref)
    half = HEAD_DIM // 2
    q_scale = HEAD_DIM ** -0.5 * math.log2(math.e)
    for rc in range(SEQ // ROW_CHUNK):
        rows = slice(rc * ROW_CHUNK, (rc + 1) * ROW_CHUNK)
        acc = jnp.dot(hn_ref[rows, :], wb_ref[...], preferred_element_type=F32)
        cos, sa, sb = rope_ref[0, rows, :], rope_ref[1, rows, :], rope_ref[2, rows, :]
        for c in range(2 * ATTN_WIDTH // LANES):
            t = acc[:, c * LANES:(c + 1) * LANES]
            if c < ATTN_WIDTH // LANES:
                t = t * q_scale
            out = t * cos + pltpu.roll(t, LANES - half, 1) * sa + pltpu.roll(t, half, 1) * sb
            o_ref[rows, c * LANES:(c + 1) * LANES] = out.astype(BF16)
        o_ref[rows, 2 * ATTN_WIDTH:] = acc[:, 2 * ATTN_WIDTH:].astype(BF16)


def _qkv_proj(hn, w_in, layer, group, rope, batch):
    const = lambda a: pl.BlockSpec(a.shape, lambda b: (0,) * a.ndim)
    assert HY_IN_WIDTH == GROUP_WIDTH
    return pl.pallas_call(
        _qkv_proj_kernel,
        grid=(batch,),
        in_specs=[pl.BlockSpec((SEQ, D_MODEL), lambda b: (b, 0)), _weight_block(layer, GROUP_WIDTH, 1 + group),
                  const(rope)],
        out_specs=pl.BlockSpec((SEQ, GROUP_WIDTH), lambda b: (b, 0)),
        out_shape=jax.ShapeDtypeStruct((batch * SEQ, GROUP_WIDTH), BF16),
        scratch_shapes=[pltpu.VMEM((D_MODEL, GROUP_WIDTH), BF16)],
        compiler_params=_params(1),
        name="qkv_proj",
    )(hn, w_in, rope)


def _hyena_kernel(x0_ref, uv_ref, skip_ref, rev_ref, fe_ref, fo_ref, gs_ref, ga_ref, h_ref, o_ref):
    lo = uv_ref[:HALF, :].astype(F32)
    mirrored = jnp.dot(rev_ref[...], uv_ref[HALF:, :], preferred_element_type=F32)
    e = (lo + mirrored).astype(BF16)
    d = (lo - mirrored).astype(BF16)
    sym = anti = None
    for blk in range(N_FREQ_BLOCKS):
        span = slice(blk * 2 * FREQ_BLOCK, (blk + 1) * 2 * FREQ_BLOCK)
        from_e = jnp.dot(fe_ref[span, :], e, preferred_element_type=F32)
        from_d = jnp.dot(fo_ref[span, :], d, preferred_element_type=F32)
        a_e, b_o = from_e[:FREQ_BLOCK], from_e[FREQ_BLOCK:]
        a_o, b_e = from_d[:FREQ_BLOCK], from_d[FREQ_BLOCK:]
        yr_e = a_e * h_ref[blk, 0] + b_e * h_ref[blk, 1]
        yi_e = a_e * h_ref[blk, 1] - b_e * h_ref[blk, 2]
        yr_o = a_o * h_ref[blk, 3] + b_o * h_ref[blk, 4]
        yi_o = a_o * h_ref[blk, 4] - b_o * h_ref[blk, 3]
        part_s = jnp.dot(gs_ref[:, span], jnp.concatenate([yr_e, yi_o], axis=0).astype(BF16),
                         preferred_element_type=F32)
        part_a = jnp.dot(ga_ref[:, span], jnp.concatenate([yi_e, yr_o], axis=0).astype(BF16),
                         preferred_element_type=F32)
        sym = part_s if sym is None else sym + part_s
        anti = part_a if anti is None else anti + part_a
    skip = skip_ref[...]
    y_lo = sym + anti + skip * lo
    y_hi = (jnp.dot(rev_ref[...], (sym - anti).astype(BF16), preferred_element_type=F32)
            + skip * uv_ref[HALF:, :].astype(F32))
    o_ref[:HALF, :] = (x0_ref[:HALF, :].astype(F32) * y_lo).astype(BF16)
    o_ref[HALF:, :] = (x0_ref[HALF:, :].astype(F32) * y_hi).astype(BF16)


def _hyena(x0c, uv, skip, fold_dft, filters, layer, batch):
    slab = pl.BlockSpec((SEQ, HYENA_WIDTH), lambda b: (b, 0))
    resident = lambda a: pl.BlockSpec(a.shape, lambda b: (0,) * a.ndim, pipeline_mode=pl.Buffered(1))
    fwd_e, fwd_o, inv_s, inv_a, rev = fold_dft
    tables = (rev, fwd_e, fwd_o, inv_s, inv_a, filters)
    layer_filters = pl.BlockSpec((None,) + filters.shape[1:], lambda b: (layer, 0, 0, 0, 0),
                                 pipeline_mode=pl.Buffered(1))
    return pl.pallas_call(
        _hyena_kernel,
        grid=(batch,),
        in_specs=[slab, slab, pl.BlockSpec((1, HYENA_WIDTH), lambda b: (0, 0))]
                 + [resident(t) for t in tables[:-1]] + [layer_filters],
        out_specs=slab,
        out_shape=jax.ShapeDtypeStruct((batch * SEQ, HYENA_WIDTH), BF16),
        compiler_params=_params(1),
        name="hyena",
    )(x0c, uv, skip.reshape(1, HYENA_WIDTH), *tables)


def _attention_kernel(cls_ref, bias_ref, *refs):
    qkv_refs = refs[:3 * N_GROUPS]
    o_ref, acc_ref, den_ref, max_ref = refs[3 * N_GROUPS:]
    lane = lax.broadcasted_iota(jnp.int32, (1, LANES), 1)
    first = lane < HEAD_DIM
    ones = jnp.ones((K_WINDOW, LANES), BF16)
    n_blocks = SEQ // Q_BLOCK

    for g, d in reversed(list(enumerate(DILATIONS))):
        q_ref, k_ref, v_ref = qkv_refs[3 * g:3 * g + 3]
        n = SEQ // d

        def q_block(c, carry, g=g, d=d, n=n, q_ref=q_ref, k_ref=k_ref, v_ref=v_ref):
            q0 = pl.multiple_of(c * Q_BLOCK, Q_BLOCK)
            ws = pl.multiple_of(jnp.clip(q0 - RADIUS, 0, SEQ - K_WINDOW), RADIUS)
            q = q_ref[pl.ds(q0, Q_BLOCK), :]
            zero = jnp.zeros_like(q)
            q2 = jnp.concatenate([jnp.where(first, q, zero), jnp.where(first, zero, q)], axis=0)
            s = lax.dot_general(q2, k_ref[pl.ds(ws, K_WINDOW), :], (((1,), (1,)), ((), ())),
                                preferred_element_type=F32)
            bias = bias_ref[cls_ref[g * n_blocks + c]]
            s = s + jnp.concatenate([bias, bias], axis=0)
            m = jnp.max(s, axis=1, keepdims=True)
            p = jnp.exp2(s - m).astype(BF16)
            out = jnp.dot(p, jnp.concatenate([v_ref[pl.ds(ws, K_WINDOW), :], ones], axis=1),
                          preferred_element_type=F32)
            acc = jnp.where(first, out[:Q_BLOCK, :LANES], out[Q_BLOCK:, :LANES])
            den = jnp.where(first, out[:Q_BLOCK, LANES:], out[Q_BLOCK:, LANES:])
            top = jnp.where(first, m[:Q_BLOCK], m[Q_BLOCK:]) + jnp.zeros((Q_BLOCK, LANES), F32)
            if d > 1:
                dst = pl.ds(q0 // n + d * lax.rem(q0, n), Q_BLOCK, stride=d)
                acc_ref[g - 1, dst, :] = acc
                den_ref[g - 1, dst, :] = den
                max_ref[g - 1, dst, :] = top
            else:
                rows = pl.ds(q0, Q_BLOCK)
                ms = [top] + [max_ref[o, rows, :] for o in range(N_GROUPS - 1)]
                accs = [acc] + [acc_ref[o, rows, :] for o in range(N_GROUPS - 1)]
                dens = [den] + [den_ref[o, rows, :] for o in range(N_GROUPS - 1)]
                peak = functools.reduce(jnp.maximum, ms)
                wts = [jnp.exp2(mm - peak) for mm in ms]
                num = sum(w * a for w, a in zip(wts, accs))
                tot = sum(w * dd for w, dd in zip(wts, dens))
                o_ref[rows, :] = (num / tot).astype(BF16)
            return carry

        lax.fori_loop(0, n_blocks, q_block, 0, unroll=n_blocks)


def _attention(qkvs, batch):
    pairs = ATTN_WIDTH // LANES
    table, ids = (jnp.asarray(a) for a in _attention_masks())
    specs = [pl.BlockSpec(memory_space=pltpu.SMEM),
             pl.BlockSpec(table.shape, lambda b, hp: (0, 0, 0))]
    operands = []
    for g in range(N_GROUPS):
        for t in range(3):
            specs.append(pl.BlockSpec((SEQ, LANES), functools.partial(lambda b, hp, t: (b, t * pairs + hp), t=t)))
            operands.append(qkvs[g])
    group_rows = pltpu.VMEM((N_GROUPS - 1, SEQ, LANES), F32)
    return pl.pallas_call(
        _attention_kernel,
        grid=(batch, pairs),
        in_specs=specs,
        out_specs=pl.BlockSpec((SEQ, LANES), lambda b, hp: (b, hp)),
        out_shape=jax.ShapeDtypeStruct((batch * SEQ, ATTN_WIDTH), BF16),
        scratch_shapes=[group_rows, group_rows, group_rows],
        compiler_params=_params(2),
        name="attention",
    )(ids, table, *operands)


def _merge_kernel(x_ref, hn_ref, yh_ref, ya_ref, wg_ref, ph_ref, pa_ref, wo_ref, o_ref):
    hn = hn_ref[...]
    g_hy = jax.nn.sigmoid(jnp.dot(hn, wg_ref[:, :D_MODEL], preferred_element_type=F32))
    merged = g_hy * jnp.dot(yh_ref[...], ph_ref[...], preferred_element_type=F32)
    g_att = jax.nn.sigmoid(jnp.dot(hn, wg_ref[:, D_MODEL:], preferred_element_type=F32))
    merged = merged + g_att * jnp.dot(ya_ref[...], pa_ref[...], preferred_element_type=F32)
    o_ref[...] = x_ref[...] + jnp.dot(merged.astype(BF16), wo_ref[...], preferred_element_type=F32)


def _merge(x2d, hn0, y_hy, y_att, w_gate, p_hy, p_att, w_o, rows):
    tm = 1024
    row = lambda w: pl.BlockSpec((tm, w), lambda i: (i, 0))
    const = lambda a: pl.BlockSpec(a.shape, lambda i: (0, 0))
    return pl.pallas_call(
        _merge_kernel,
        grid=(rows // tm,),
        in_specs=[row(D_MODEL), row(D_MODEL), row(HYENA_WIDTH), row(ATTN_WIDTH),
                  const(w_gate), const(p_hy), const(p_att), const(w_o)],
        out_specs=row(D_MODEL),
        out_shape=jax.ShapeDtypeStruct((rows, D_MODEL), F32),
        compiler_params=_params(1),
        name="merge",
    )(x2d, hn0, y_hy, y_att, w_gate, p_hy, p_att, w_o)


def _ffn_kernel(x_ref, g_ref, w1_ref, w2_ref, gn_ref, *rest, last):
    x = x_ref[...]
    hn = _rms(x, g_ref[...]).astype(BF16)
    chunk = 1024
    y = x
    for c in range(D_FF // chunk):
        h = jnp.dot(hn, w1_ref[:, c * chunk:(c + 1) * chunk], preferred_element_type=F32)
        h = jnp.square(jnp.maximum(h, 0.0)).astype(BF16)
        y = y + jnp.dot(h, w2_ref[c * chunk:(c + 1) * chunk, :], preferred_element_type=F32)
    if last:
        (o_ref,) = rest
        o_ref[...] = _rms(y, gn_ref[...])
    else:
        o_ref, o0_ref, o1_ref, o2_ref, a_ref, b_ref = rest
        o_ref[...] = y
        _store_row_orders(_rms(y, gn_ref[...]), o0_ref, o1_ref, o2_ref, a_ref, b_ref)


def _ffn(x2d, gain, w1, w2, gain_next, batch, last):
    rows = batch * SEQ
    tile = 2 * ROW_TILE if last else ROW_TILE
    row = pl.BlockSpec((tile, D_MODEL), lambda i: (i, 0))
    const = lambda a: pl.BlockSpec(a.shape, lambda i: (0, 0))
    gain = gain.reshape(1, D_MODEL)
    gain_next = gain_next.reshape(1, D_MODEL)
    out_shape = [jax.ShapeDtypeStruct((rows, D_MODEL), F32)]
    out_specs = [row]
    scratch = []
    if not last:
        shapes, specs, scratch = _row_order_outputs(batch, tile)
        out_shape += shapes
        out_specs += specs
    return pl.pallas_call(
        functools.partial(_ffn_kernel, last=last),
        grid=(rows // tile,),
        in_specs=[row, const(gain), const(w1), const(w2), const(gain_next)],
        out_specs=out_specs,
        out_shape=out_shape,
        scratch_shapes=scratch,
        compiler_params=_params(1),
        name="ffn",
    )(x2d, gain, w1, w2, gain_next)


def kernel(x, norm_mix, w_in, conv_w, conv_b, filt_w1, filt_b1, filt_w_inner, filt_b_inner, filt_w_out,
           filt_freq, hy_skip, p_hy, p_att, w_o, norm_ffn, w_ff1, w_ff2, norm_final):
    batch, seq, d_model = x.shape
    assert (seq, d_model) == (SEQ, D_MODEL)
    depth = norm_mix.shape[0]
    rows = batch * seq
    bf = lambda a: a.astype(BF16)
    rope = tuple(jnp.asarray(t) for t in _rope_tables())
    fold_dft = tuple(bf(jnp.asarray(t)) for t in _fold_dft_tables())
    filter_dft = bf(jnp.asarray(_filter_dft_table()))
    z, window = (jnp.asarray(t) for t in _filter_tables())

    x2d = x.reshape(rows, d_model)
    hns, filters = _prologue(x2d, norm_mix[0], z, window, filt_w1, filt_b1, filt_w_inner, filt_b_inner, filt_w_out,
                             filt_freq, filter_dft, batch)
    for l in range(depth):
        hns = [h.reshape(rows, d_model) for h in hns]
        x0c, uv = _hy_proj(hns[0], w_in, l, conv_w[l], conv_b[l], batch)
        y_hy = _hyena(x0c, uv, hy_skip[l], fold_dft, filters, l, batch)
        qkvs = [_qkv_proj(hns[g], w_in, l, g, rope[g], batch) for g in range(N_GROUPS)]
        y_att = _attention(qkvs, batch)
        x2d = _merge(x2d, hns[0], y_hy, y_att, bf(w_in[l, :, HY_IN_WIDTH + QKV_WIDTH:]), bf(p_hy[l]), bf(p_att[l]),
                     bf(w_o[l]), rows)
        last = l == depth - 1
        outs = _ffn(x2d, norm_ffn[l], bf(w_ff1[l]), bf(w_ff2[l]), norm_final if last else norm_mix[l + 1], batch, last)
        x2d, hns = outs[0], outs[1:]
    return x2d.reshape(batch, seq, d_model)
```

```python
import functools
import math

import numpy as np
import jax
import jax.numpy as jnp
from jax import lax
from jax.experimental import pallas as pl
from jax.experimental.pallas import tpu as pltpu

F32 = jnp.float32
BF16 = jnp.bfloat16

D_MODEL = 1024
SEQ = 2048
HEAD_DIM = 64
ATTN_HEADS = 8
DILATIONS = (1, 4, 16)
RADIUS = 64
N_GROUPS = len(DILATIONS)
ATTN_WIDTH = ATTN_HEADS * HEAD_DIM
ROPE_THETA = 10000.0
NEG_INF = -1e30
HYENA_WIDTH = D_MODEL // 2
FILTER_BANDS = 16
FILTER_EMB_DIM = 1 + 2 * FILTER_BANDS
FILTER_HIDDEN = 64
FILTER_INNER = 2
DECAY_TARGET = 1e-2
FAST_DECAY_PCT = 0.3
SLOW_DECAY_PCT = 1.5
HY_IN_WIDTH = 3 * HYENA_WIDTH
GROUP_WIDTH = 3 * ATTN_WIDTH
QKV_WIDTH = N_GROUPS * GROUP_WIDTH
D_FF = 4 * D_MODEL
RMS_EPS = 1e-6

LANES = 128
SUBLANES = 8
N_SLABS = D_MODEL // LANES
N_FFT = 2 * SEQ
HALF = SEQ // 2
FREQ_BLOCK = 512
N_FREQ_BLOCKS = HALF // FREQ_BLOCK
N_FILTER_SETS = 5
FILTER_CHUNK = 128
MLP_ROWS = 256
PHASE_STEPS = SEQ // MLP_ROWS
assert PHASE_STEPS == HALF // FILTER_CHUNK and FREQ_BLOCK % FILTER_CHUNK == 0
Q_BLOCK = 128
K_WINDOW = 256
ROW_TILE = 512
ROW_CHUNK = 512
VMEM_LIMIT = 56 * 1024 * 1024

assert DILATIONS[0] == 1 and DILATIONS[2] == DILATIONS[1] ** 2


def _params(n_axes, vmem=VMEM_LIMIT):
    return pltpu.CompilerParams(dimension_semantics=("arbitrary",) * n_axes, vmem_limit_bytes=vmem)


def _perm_positions(dilation):
    n = SEQ // dilation
    r = np.arange(SEQ) // n
    m = np.arange(SEQ) % n
    return m * dilation + r


@functools.lru_cache(maxsize=None)
def _rope_tables():
    half = HEAD_DIM // 2
    inv_freq = ROPE_THETA ** (-np.arange(half, dtype=np.float64) / half)
    lane = np.arange(LANES)
    first_half = (lane % HEAD_DIM) < half
    tables = []
    for d in DILATIONS:
        ang = _perm_positions(d)[:, None].astype(np.float64) * inv_freq[None, :]
        cos_l = np.cos(ang)[:, lane % half]
        sin_l = np.sin(ang)[:, lane % half]
        sa = np.where(first_half[None, :], -sin_l, 0.0)
        sb = np.where(first_half[None, :], 0.0, sin_l)
        tables.append(np.stack([cos_l, sa, sb]).astype(np.float32))
    return tuple(tables)


def _blocked(rows_a, rows_b, axis):
    parts = []
    for b in range(N_FREQ_BLOCKS):
        sl = [slice(None)] * rows_a.ndim
        sl[axis] = slice(b * FREQ_BLOCK, (b + 1) * FREQ_BLOCK)
        parts += [rows_a[tuple(sl)], rows_b[tuple(sl)]]
    return np.concatenate(parts, axis=axis)


@functools.lru_cache(maxsize=None)
def _fold_dft_tables():
    j = np.arange(HALF, dtype=np.int64)[:, None]
    n = np.arange(HALF, dtype=np.int64)[None, :]
    ang = lambda k: ((k * (2 * n + 1)) % (2 * N_FFT)).astype(np.float64) * (2.0 * math.pi / (2 * N_FFT))
    ce, se, co, so = np.cos(ang(2 * j)), np.sin(ang(2 * j)), np.cos(ang(2 * j + 1)), np.sin(ang(2 * j + 1))
    se[0, :] = (-1.0) ** np.arange(HALF)
    w_e = np.full((HALF, 1), 2.0 / N_FFT)
    w_e[0] = 1.0 / N_FFT
    w_o = 2.0 / N_FFT
    fwd_e = _blocked(ce, so, 0)
    fwd_o = _blocked(co, se, 0)
    inv_s = _blocked((w_e * ce).T, (-w_o * so).T, 1)
    inv_a = _blocked((-w_e * se).T, (w_o * co).T, 1)
    rev = np.eye(HALF)[::-1]
    return tuple(np.ascontiguousarray(t).astype(np.float32) for t in (fwd_e, fwd_o, inv_s, inv_a, rev))


@functools.lru_cache(maxsize=None)
def _filter_dft_table():
    j = np.arange(HALF, dtype=np.int64)[:, None]
    lag = np.arange(SEQ, dtype=np.int64)[None, :]
    ang = lambda k: ((k * lag) % N_FFT).astype(np.float64) * (2.0 * math.pi / N_FFT)
    ce, se, co, so = np.cos(ang(2 * j)), np.sin(ang(2 * j)), np.cos(ang(2 * j + 1)), np.sin(ang(2 * j + 1))
    se[0, :] = (-1.0) ** np.arange(SEQ)
    parts = []
    for c in range(HALF // FILTER_CHUNK):
        rows = slice(c * FILTER_CHUNK, (c + 1) * FILTER_CHUNK)
        parts += [ce[rows], se[rows], co[rows], so[rows]]
    return np.concatenate(parts, axis=0).astype(np.float32)


@functools.lru_cache(maxsize=None)
def _filter_tables():
    n = np.arange(SEQ, dtype=np.float64)
    t = n / max(SEQ - 1, 1)
    bands = np.linspace(1e-4, FILTER_BANDS - 1, FILTER_BANDS)
    ang = (2.0 * math.pi / SEQ) * n[:, None] * bands[None, :]
    z = np.concatenate([t[:, None], np.cos(ang), -np.sin(ang)], axis=-1)
    z = np.pad(z, ((0, 0), (0, LANES - FILTER_EMB_DIM)))
    max_decay = math.log(DECAY_TARGET) / FAST_DECAY_PCT
    min_decay = math.log(DECAY_TARGET) / SLOW_DECAY_PCT
    deltas = np.abs(np.linspace(min_decay, max_decay, HYENA_WIDTH))
    window = np.exp(-t[:, None] * deltas[None, :])
    return z.astype(np.float32), window.astype(np.float32)


@functools.lru_cache(maxsize=None)
def _attention_masks():
    i = np.arange(Q_BLOCK)[:, None]
    j = np.arange(K_WINDOW)[None, :]
    classes, ids = {}, []
    for d in DILATIONS:
        n = SEQ // d
        for c in range(SEQ // Q_BLOCK):
            q0 = c * Q_BLOCK
            ws = min(max(q0 - RADIUS, 0), SEQ - K_WINDOW)
            seg = (q0 // n) * n
            kpos = ws + j
            valid = (np.abs(kpos - (q0 + i)) <= RADIUS) & (kpos >= seg) & (kpos < seg + n)
            bias = np.where(valid, 0.0, NEG_INF).astype(np.float32)
            ids.append(classes.setdefault(bias.tobytes(), len(classes)))
    table = np.stack([np.frombuffer(b, np.float32).reshape(Q_BLOCK, K_WINDOW) for b in classes])
    return table, np.asarray(ids, np.int32)


def _rms(x, gain):
    return x * lax.rsqrt(jnp.mean(x * x, axis=-1, keepdims=True) + RMS_EPS) * gain


def _store_row_orders(hn, o0_ref, o1_ref, o2_ref, a_ref, b_ref):
    d1, d2 = DILATIONS[1], DILATIONS[2]
    n1, n2 = hn.shape[0] // d1, hn.shape[0] // d2
    o0_ref[...] = hn.astype(BF16)
    for s in range(N_SLABS):
        a_ref[s] = hn[:, s * LANES:(s + 1) * LANES]
    for r in range(d1):
        v = jnp.concatenate([a_ref[s, pl.ds(r, n1, stride=d1), :] for s in range(N_SLABS)], axis=1)
        o1_ref[r] = v.astype(BF16)
        for s in range(N_SLABS):
            b_ref[s, r * n1:(r + 1) * n1, :] = v[:, s * LANES:(s + 1) * LANES]
    for r in range(d2):
        src = pl.ds((r % d1) * n1 + r // d1, n2, stride=d1)
        v = jnp.concatenate([b_ref[s, src, :] for s in range(N_SLABS)], axis=1)
        o2_ref[r] = v.astype(BF16)


def _row_order_outputs(batch, tile):
    d1, d2 = DILATIONS[1], DILATIONS[2]
    tiles = SEQ // tile
    shapes = [jax.ShapeDtypeStruct((batch * SEQ, D_MODEL), BF16),
              jax.ShapeDtypeStruct((batch, d1, SEQ // d1, D_MODEL), BF16),
              jax.ShapeDtypeStruct((batch, d2, SEQ // d2, D_MODEL), BF16)]
    specs = [pl.BlockSpec((tile, D_MODEL), lambda i: (i, 0)),
             pl.BlockSpec((None, d1, tile // d1, D_MODEL), lambda i: (i // tiles, 0, i % tiles, 0)),
             pl.BlockSpec((None, d2, tile // d2, D_MODEL), lambda i: (i // tiles, 0, i % tiles, 0))]
    scratch = [pltpu.VMEM((N_SLABS, tile, LANES), F32), pltpu.VMEM((N_SLABS, tile, LANES), F32)]
    return shapes, specs, scratch


def _filter_step(i, depth):
    phase = i // PHASE_STEPS
    sub = i % PHASE_STEPS
    active = phase < 2 * depth
    layer = jnp.minimum(phase // 2, depth - 1)
    is_mlp = active & (phase % 2 == 0)
    is_spec = active & (phase % 2 == 1)
    chunk = jnp.where(is_spec, sub, jnp.where(active, 0, PHASE_STEPS - 1))
    return layer, is_mlp, is_spec, sub, chunk


def _prologue_kernel(x_ref, g_ref, z_ref, win_ref, w1_ref, b1_ref, wi_ref, bi_ref, wo_ref, fr_ref, kf_ref,
                     o0_ref, o1_ref, o2_ref, h_ref, a_ref, b_ref, hcat_ref, *, depth, n_tiles):
    i = pl.program_id(0)
    hi = lax.Precision.HIGHEST
    width = HYENA_WIDTH

    @pl.when(i < n_tiles)
    def _():
        _store_row_orders(_rms(x_ref[...], g_ref[...]), o0_ref, o1_ref, o2_ref, a_ref, b_ref)

    layer, is_mlp, is_spec, sub, _ = _filter_step(i, depth)

    @pl.when(is_mlp)
    def _():
        half_rows = MLP_ROWS // 2
        r0 = pl.multiple_of(sub * MLP_ROWS, MLP_ROWS)
        halves = (pl.ds(r0, half_rows), pl.ds(r0 + half_rows, half_rows))
        fr = fr_ref[layer]
        zz = jnp.concatenate([z_ref[halves[0], :], z_ref[halves[1], :]], axis=1)
        hid = jnp.sin(fr * (jnp.dot(zz, w1_ref[layer], precision=hi, preferred_element_type=F32) + b1_ref[layer]))
        for l in range(FILTER_INNER):
            hid = jnp.sin(fr * (jnp.dot(hid, wi_ref[layer, l], precision=hi,
                                        preferred_element_type=F32) + bi_ref[layer, l]))
        hid = hid.astype(BF16)
        for h, rows in enumerate(halves):
            filt = jnp.dot(hid, wo_ref[layer, h].astype(BF16), preferred_element_type=F32)
            win = win_ref[rows, :]
            row = r0 + h * half_rows + lax.broadcasted_iota(jnp.int32, (half_rows, 1), 0)
            h_fwd = filt[:, :width] * win
            h_bwd = jnp.where(row == 0, 0.0, filt[:, width:] * win)
            hcat_ref[layer, rows, :width] = h_fwd.astype(BF16)
            hcat_ref[layer, rows, width:] = h_bwd.astype(BF16)

    @pl.when(is_spec)
    def _():
        sp = jnp.dot(kf_ref[...], hcat_ref[layer], preferred_element_type=F32)
        f, b = sp[:, :width], sp[:, width:]
        sets = [slice(g * FILTER_CHUNK, (g + 1) * FILTER_CHUNK) for g in range(4)]
        first_row = (lax.broadcasted_iota(jnp.int32, (FILTER_CHUNK, 1), 0) == 0) & (sub == 0)
        re_even = f[sets[0]] + b[sets[0]]
        nyquist = f[sets[1]][0:1] + b[sets[1]][0:1]
        h_ref[0] = re_even
        h_ref[1] = jnp.where(first_row, 0.0, b[sets[1]] - f[sets[1]])
        h_ref[2] = jnp.where(first_row, nyquist, re_even)
        h_ref[3] = f[sets[2]] + b[sets[2]]
        h_ref[4] = b[sets[3]] - f[sets[3]]


def _prologue(x2d, gain, z, window, w1, b1, w_inner, b_inner, w_out, freq, filter_dft, batch):
    depth = w1.shape[0]
    tile = 2 * ROW_TILE
    n_tiles = batch * SEQ // tile
    steps = max(n_tiles, 2 * depth * PHASE_STEPS)
    resident = lambda a: pl.BlockSpec(a.shape, lambda i: (0,) * a.ndim, pipeline_mode=pl.Buffered(1))
    assert 2 * FILTER_HIDDEN == LANES
    zero = jnp.zeros_like
    diag2 = lambda a: jnp.concatenate([jnp.concatenate([a, zero(a)], axis=-1),
                                       jnp.concatenate([zero(a), a], axis=-1)], axis=-2)
    twice = lambda a: jnp.concatenate([a, a], axis=-1)
    w1p = diag2(jnp.pad(w1, ((0, 0), (0, LANES - FILTER_EMB_DIM), (0, 0))))
    b1 = twice(b1.reshape(depth, 1, FILTER_HIDDEN))
    w_inner = diag2(w_inner)
    b_inner = twice(b_inner.reshape(depth, FILTER_INNER, 1, FILTER_HIDDEN))
    freq = twice(freq.reshape(depth, 1, FILTER_HIDDEN))
    w_out = jnp.stack([jnp.concatenate([w_out, zero(w_out)], axis=1),
                       jnp.concatenate([zero(w_out), w_out], axis=1)], axis=1)
    consts = (z, window, w1p, b1, w_inner, b_inner, w_out, freq)
    shapes, specs, scratch = _row_order_outputs(batch, tile)
    tile_of = lambda i: jnp.minimum(i, n_tiles - 1)
    specs = [pl.BlockSpec(s.block_shape, functools.partial(lambda i, m: m(tile_of(i)), m=s.index_map)) for s in specs]
    per_block = FREQ_BLOCK // FILTER_CHUNK

    def filter_window(i):
        layer, _, _, _, chunk = _filter_step(i, depth)
        return layer, chunk // per_block, 0, chunk % per_block, 0

    outs = pl.pallas_call(
        functools.partial(_prologue_kernel, depth=depth, n_tiles=n_tiles),
        grid=(steps,),
        in_specs=[pl.BlockSpec((tile, D_MODEL), lambda i: (tile_of(i), 0)),
                  pl.BlockSpec((1, D_MODEL), lambda i: (0, 0))] + [resident(a) for a in consts]
                 + [pl.BlockSpec((4 * FILTER_CHUNK, SEQ), lambda i: (_filter_step(i, depth)[4], 0))],
        out_specs=specs + [pl.BlockSpec((None, None, N_FILTER_SETS, FILTER_CHUNK, HYENA_WIDTH), filter_window)],
        out_shape=shapes + [jax.ShapeDtypeStruct((depth, N_FREQ_BLOCKS, N_FILTER_SETS, FREQ_BLOCK, HYENA_WIDTH), F32)],
        scratch_shapes=scratch + [pltpu.VMEM((depth, SEQ, 2 * HYENA_WIDTH), BF16)],
        compiler_params=_params(1),
        name="prologue",
    )(x2d, gain.reshape(1, D_MODEL), *consts, filter_dft)
    return outs[:3], outs[3]


def _hy_proj_kernel(hn_ref, w_ref, cw_ref, cb_ref, x0_ref, uv_ref, wb_ref):
    _cast_weights_once(w_ref, wb_ref)
    n_chunks = SEQ // ROW_CHUNK
    pre = [jnp.dot(hn_ref[rc * ROW_CHUNK:(rc + 1) * ROW_CHUNK, :], wb_ref[...], preferred_element_type=F32)
           for rc in range(n_chunks)]
    w = cw_ref[...]
    bias = cb_ref[...]
    zeros = jnp.zeros((SUBLANES, HY_IN_WIDTH), F32)
    for rc in range(n_chunks):
        before = pre[rc - 1][ROW_CHUNK - SUBLANES:] if rc > 0 else zeros
        after = pre[rc + 1][:SUBLANES] if rc < n_chunks - 1 else zeros
        ext = ROW_CHUNK + SUBLANES
        prev = pltpu.roll(jnp.concatenate([before, pre[rc]], axis=0), 1, 0)[SUBLANES:]
        nxt = pltpu.roll(jnp.concatenate([pre[rc], after], axis=0), ext - 1, 0)[:ROW_CHUNK]
        u = bias + prev * w[0:1] + pre[rc] * w[1:2] + nxt * w[2:3]
        rows = slice(rc * ROW_CHUNK, (rc + 1) * ROW_CHUNK)
        x0_ref[rows, :] = u[:, :HYENA_WIDTH].astype(BF16)
        uv_ref[rows, :] = (u[:, HYENA_WIDTH:2 * HYENA_WIDTH] * u[:, 2 * HYENA_WIDTH:]).astype(BF16)


def _hy_proj(hn0, w_in, layer, conv_w, conv_b, batch):
    const = lambda a: pl.BlockSpec(a.shape, lambda b: (0,) * a.ndim)
    conv_b = conv_b.reshape(1, HY_IN_WIDTH)
    out_spec = pl.BlockSpec((SEQ, HYENA_WIDTH), lambda b: (b, 0))
    out_shape = jax.ShapeDtypeStruct((batch * SEQ, HYENA_WIDTH), BF16)
    return pl.pallas_call(
        _hy_proj_kernel,
        grid=(batch,),
        in_specs=[pl.BlockSpec((SEQ, D_MODEL), lambda b: (b, 0)), _weight_block(layer, HY_IN_WIDTH, 0),
                  const(conv_w), const(conv_b)],
        out_specs=[out_spec, out_spec],
        out_shape=[out_shape, out_shape],
        scratch_shapes=[pltpu.VMEM((D_MODEL, HY_IN_WIDTH), BF16)],
        compiler_params=_params(1),
        name="hy_proj",
    )(hn0, w_in, conv_w, conv_b)


def _cast_weights_once(w_ref, wb_ref):
    @pl.when(pl.program_id(0) == 0)
    def _():
        wb_ref[...] = w_ref[...].astype(BF16)


def _weight_block(layer, width, block):
    return pl.BlockSpec((None, D_MODEL, width), lambda b: (layer, 0, block), pipeline_mode=pl.Buffered(1))


def _qkv_proj_kernel(hn_ref, w_ref, rope_ref, o_ref, wb_ref):
    _cast_weights_once(w_ref, wb_ref)
    half = HEAD_DIM // 2
    q_scale = HEAD_DIM ** -0.5 * math.log2(math.e)
    for rc in range(SEQ // ROW_CHUNK):
        rows = slice(rc * ROW_CHUNK, (rc + 1) * ROW_CHUNK)
        acc = jnp.dot(hn_ref[rows, :], wb_ref[...], preferred_element_type=F32)
        cos, sa, sb = rope_ref[0, rows, :], rope_ref[1, rows, :], rope_ref[2, rows, :]
        for c in range(2 * ATTN_WIDTH // LANES):
            t = acc[:, c * LANES:(c + 1) * LANES]
            if c < ATTN_WIDTH // LANES:
                t = t * q_scale
            out = t * cos + pltpu.roll(t, LANES - half, 1) * sa + pltpu.roll(t, half, 1) * sb
            o_ref[rows, c * LANES:(c + 1) * LANES] = out.astype(BF16)
        o_ref[rows, 2 * ATTN_WIDTH:] = acc[:, 2 * ATTN_WIDTH:].astype(BF16)


def _qkv_proj(hn, w_in, layer, group, rope, batch):
    const = lambda a: pl.BlockSpec(a.shape, lambda b: (0,) * a.ndim)
    assert HY_IN_WIDTH == GROUP_WIDTH
    return pl.pallas_call(
        _qkv_proj_kernel,
        grid=(batch,),
        in_specs=[pl.BlockSpec((SEQ, D_MODEL), lambda b: (b, 0)), _weight_block(layer, GROUP_WIDTH, 1 + group),
                  const(rope)],
        out_specs=pl.BlockSpec((SEQ, GROUP_WIDTH), lambda b: (b, 0)),
        out_shape=jax.ShapeDtypeStruct((batch * SEQ, GROUP_WIDTH), BF16),
        scratch_shapes=[pltpu.VMEM((D_MODEL, GROUP_WIDTH), BF16)],
        compiler_params=_params(1),
        name="qkv_proj",
    )(hn, w_in, rope)


def _hyena_kernel(x0_ref, uv_ref, skip_ref, rev_ref, fe_ref, fo_ref, gs_ref, ga_ref, h_ref, o_ref):
    lo = uv_ref[:HALF, :].astype(F32)
    mirrored = jnp.dot(rev_ref[...], uv_ref[HALF:, :], preferred_element_type=F32)
    e = (lo + mirrored).astype(BF16)
    d = (lo - mirrored).astype(BF16)
    sym = anti = None
    for blk in range(N_FREQ_BLOCKS):
        span = slice(blk * 2 * FREQ_BLOCK, (blk + 1) * 2 * FREQ_BLOCK)
        from_e = jnp.dot(fe_ref[span, :], e, preferred_element_type=F32)
        from_d = jnp.dot(fo_ref[span, :], d, preferred_element_type=F32)
        a_e, b_o = from_e[:FREQ_BLOCK], from_e[FREQ_BLOCK:]
        a_o, b_e = from_d[:FREQ_BLOCK], from_d[FREQ_BLOCK:]
        yr_e = a_e * h_ref[blk, 0] + b_e * h_ref[blk, 1]
        yi_e = a_e * h_ref[blk, 1] - b_e * h_ref[blk, 2]
        yr_o = a_o * h_ref[blk, 3] + b_o * h_ref[blk, 4]
        yi_o = a_o * h_ref[blk, 4] - b_o * h_ref[blk, 3]
        part_s = jnp.dot(gs_ref[:, span], jnp.concatenate([yr_e, yi_o], axis=0).astype(BF16),
                         preferred_element_type=F32)
        part_a = jnp.dot(ga_ref[:, span], jnp.concatenate([yi_e, yr_o], axis=0).astype(BF16),
                         preferred_element_type=F32)
        sym = part_s if sym is None else sym + part_s
        anti = part_a if anti is None else anti + part_a
    skip = skip_ref[...]
    y_lo = sym + anti + skip * lo
    y_hi = (jnp.dot(rev_ref[...], (sym - anti).astype(BF16), preferred_element_type=F32)
            + skip * uv_ref[HALF:, :].astype(F32))
    o_ref[:HALF, :] = (x0_ref[:HALF, :].astype(F32) * y_lo).astype(BF16)
    o_ref[HALF:, :] = (x0_ref[HALF:, :].astype(F32) * y_hi).astype(BF16)


def _hyena(x0c, uv, skip, fold_dft, filters, layer, batch):
    slab = pl.BlockSpec((SEQ, HYENA_WIDTH), lambda b: (b, 0))
    resident = lambda a: pl.BlockSpec(a.shape, lambda b: (0,) * a.ndim, pipeline_mode=pl.Buffered(1))
    fwd_e, fwd_o, inv_s, inv_a, rev = fold_dft
    tables = (rev, fwd_e, fwd_o, inv_s, inv_a, filters)
    layer_filters = pl.BlockSpec((None,) + filters.shape[1:], lambda b: (layer, 0, 0, 0, 0),
                                 pipeline_mode=pl.Buffered(1))
    return pl.pallas_call(
        _hyena_kernel,
        grid=(batch,),
        in_specs=[slab, slab, pl.BlockSpec((1, HYENA_WIDTH), lambda b: (0, 0))]
                 + [resident(t) for t in tables[:-1]] + [layer_filters],
        out_specs=slab,
        out_shape=jax.ShapeDtypeStruct((batch * SEQ, HYENA_WIDTH), BF16),
        compiler_params=_params(1),
        name="hyena",
    )(x0c, uv, skip.reshape(1, HYENA_WIDTH), *tables)


def _attention_kernel(cls_ref, bias_ref, *refs):
    qkv_refs = refs[:3 * N_GROUPS]
    o_ref, acc_ref, den_ref, max_ref = refs[3 * N_GROUPS:]
    lane = lax.broadcasted_iota(jnp.int32, (1, LANES), 1)
    first = lane < HEAD_DIM
    ones = jnp.ones((K_WINDOW, LANES), BF16)
    n_blocks = SEQ // Q_BLOCK

    for g, d in reversed(list(enumerate(DILATIONS))):
        q_ref, k_ref, v_ref = qkv_refs[3 * g:3 * g + 3]
        n = SEQ // d

        def q_block(c, carry, g=g, d=d, n=n, q_ref=q_ref, k_ref=k_ref, v_ref=v_ref):
            q0 = pl.multiple_of(c * Q_BLOCK, Q_BLOCK)
            ws = pl.multiple_of(jnp.clip(q0 - RADIUS, 0, SEQ - K_WINDOW), RADIUS)
            q = q_ref[pl.ds(q0, Q_BLOCK), :]
            zero = jnp.zeros_like(q)
            q2 = jnp.concatenate([jnp.where(first, q, zero), jnp.where(first, zero, q)], axis=0)
            s = lax.dot_general(q2, k_ref[pl.ds(ws, K_WINDOW), :], (((1,), (1,)), ((), ())),
                                preferred_element_type=F32)
            bias = bias_ref[cls_ref[g * n_blocks + c]]
            s = s + jnp.concatenate([bias, bias], axis=0)
            m = jnp.max(s, axis=1, keepdims=True)
            p = jnp.exp2(s - m).astype(BF16)
            out = jnp.dot(p, jnp.concatenate([v_ref[pl.ds(ws, K_WINDOW), :], ones], axis=1),
                          preferred_element_type=F32)
            acc = jnp.where(first, out[:Q_BLOCK, :LANES], out[Q_BLOCK:, :LANES])
            den = jnp.where(first, out[:Q_BLOCK, LANES:], out[Q_BLOCK:, LANES:])
            top = jnp.where(first, m[:Q_BLOCK], m[Q_BLOCK:]) + jnp.zeros((Q_BLOCK, LANES), F32)
            if d > 1:
                dst = pl.ds(q0 // n + d * lax.rem(q0, n), Q_BLOCK, stride=d)
                acc_ref[g - 1, dst, :] = acc
                den_ref[g - 1, dst, :] = den
                max_ref[g - 1, dst, :] = top
            else:
                rows = pl.ds(q0, Q_BLOCK)
                ms = [top] + [max_ref[o, rows, :] for o in range(N_GROUPS - 1)]
                accs = [acc] + [acc_ref[o, rows, :] for o in range(N_GROUPS - 1)]
                dens = [den] + [den_ref[o, rows, :] for o in range(N_GROUPS - 1)]
                peak = functools.reduce(jnp.maximum, ms)
                wts = [jnp.exp2(mm - peak) for mm in ms]
                num = sum(w * a for w, a in zip(wts, accs))
                tot = sum(w * dd for w, dd in zip(wts, dens))
                o_ref[rows, :] = (num / tot).astype(BF16)
            return carry

        lax.fori_loop(0, n_blocks, q_block, 0, unroll=n_blocks)


def _attention(qkvs, batch):
    pairs = ATTN_WIDTH // LANES
    table, ids = (jnp.asarray(a) for a in _attention_masks())
    specs = [pl.BlockSpec(memory_space=pltpu.SMEM),
             pl.BlockSpec(table.shape, lambda b, hp: (0, 0, 0))]
    operands = []
    for g in range(N_GROUPS):
        for t in range(3):
            specs.append(pl.BlockSpec((SEQ, LANES), functools.partial(lambda b, hp, t: (b, t * pairs + hp), t=t)))
            operands.append(qkvs[g])
    group_rows = pltpu.VMEM((N_GROUPS - 1, SEQ, LANES), F32)
    return pl.pallas_call(
        _attention_kernel,
        grid=(batch, pairs),
        in_specs=specs,
        out_specs=pl.BlockSpec((SEQ, LANES), lambda b, hp: (b, hp)),
        out_shape=jax.ShapeDtypeStruct((batch * SEQ, ATTN_WIDTH), BF16),
        scratch_shapes=[group_rows, group_rows, group_rows],
        compiler_params=_params(2),
        name="attention",
    )(ids, table, *operands)


def _merge_kernel(x_ref, hn_ref, yh_ref, ya_ref, wg_ref, ph_ref, pa_ref, wo_ref, o_ref):
    hn = hn_ref[...]
    g_hy = jax.nn.sigmoid(jnp.dot(hn, wg_ref[:, :D_MODEL], preferred_element_type=F32))
    merged = g_hy * jnp.dot(yh_ref[...], ph_ref[...], preferred_element_type=F32)
    g_att = jax.nn.sigmoid(jnp.dot(hn, wg_ref[:, D_MODEL:], preferred_element_type=F32))
    merged = merged + g_att * jnp.dot(ya_ref[...], pa_ref[...], preferred_element_type=F32)
    o_ref[...] = x_ref[...] + jnp.dot(merged.astype(BF16), wo_ref[...], preferred_element_type=F32)


def _merge(x2d, hn0, y_hy, y_att, w_gate, p_hy, p_att, w_o, rows):
    tm = 1024
    row = lambda w: pl.BlockSpec((tm, w), lambda i: (i, 0))
    const = lambda a: pl.BlockSpec(a.shape, lambda i: (0, 0))
    return pl.pallas_call(
        _merge_kernel,
        grid=(rows // tm,),
        in_specs=[row(D_MODEL), row(D_MODEL), row(HYENA_WIDTH), row(ATTN_WIDTH),
                  const(w_gate), const(p_hy), const(p_att), const(w_o)],
        out_specs=row(D_MODEL),
        out_shape=jax.ShapeDtypeStruct((rows, D_MODEL), F32),
        compiler_params=_params(1),
        name="merge",
    )(x2d, hn0, y_hy, y_att, w_gate, p_hy, p_att, w_o)


def _ffn_kernel(x_ref, g_ref, w1_ref, w2_ref, gn_ref, *rest, last):
    x = x_ref[...]
    hn = _rms(x, g_ref[...]).astype(BF16)
    chunk = 1024
    y = x
    for c in range(D_FF // chunk):
        h = jnp.dot(hn, w1_ref[:, c * chunk:(c + 1) * chunk], preferred_element_type=F32)
        h = jnp.square(jnp.maximum(h, 0.0)).astype(BF16)
        y = y + jnp.dot(h, w2_ref[c * chunk:(c + 1) * chunk, :], preferred_element_type=F32)
    if last:
        (o_ref,) = rest
        o_ref[...] = _rms(y, gn_ref[...])
    else:
        o_ref, o0_ref, o1_ref, o2_ref, a_ref, b_ref = rest
        o_ref[...] = y
        _store_row_orders(_rms(y, gn_ref[...]), o0_ref, o1_ref, o2_ref, a_ref, b_ref)


def _ffn(x2d, gain, w1, w2, gain_next, batch, last):
    rows = batch * SEQ
    tile = 2 * ROW_TILE if last else ROW_TILE
    row = pl.BlockSpec((tile, D_MODEL), lambda i: (i, 0))
    const = lambda a: pl.BlockSpec(a.shape, lambda i: (0, 0))
    gain = gain.reshape(1, D_MODEL)
    gain_next = gain_next.reshape(1, D_MODEL)
    out_shape = [jax.ShapeDtypeStruct((rows, D_MODEL), F32)]
    out_specs = [row]
    scratch = []
    if not last:
        shapes, specs, scratch = _row_order_outputs(batch, tile)
        out_shape += shapes
        out_specs += specs
    return pl.pallas_call(
        functools.partial(_ffn_kernel, last=last),
        grid=(rows // tile,),
        in_specs=[row, const(gain), const(w1), const(w2), const(gain_next)],
        out_specs=out_specs,
        out_shape=out_shape,
        scratch_shapes=scratch,
        compiler_params=_params(1),
        name="ffn",
    )(x2d, gain, w1, w2, gain_next)


def kernel(x, norm_mix, w_in, conv_w, conv_b, filt_w1, filt_b1, filt_w_inner, filt_b_inner, filt_w_out,
           filt_freq, hy_skip, p_hy, p_att, w_o, norm_ffn, w_ff1, w_ff2, norm_final):
    batch, seq, d_model = x.shape
    assert (seq, d_model) == (SEQ, D_MODEL)
    depth = norm_mix.shape[0]
    rows = batch * seq
    bf = lambda a: a.astype(BF16)
    rope = tuple(jnp.asarray(t) for t in _rope_tables())
    fold_dft = tuple(bf(jnp.asarray(t)) for t in _fold_dft_tables())
    filter_dft = bf(jnp.asarray(_filter_dft_table()))
    z, window = (jnp.asarray(t) for t in _filter_tables())

    x2d = x.reshape(rows, d_model)
    hns, filters = _prologue(x2d, norm_mix[0], z, window, filt_w1, filt_b1, filt_w_inner, filt_b_inner, filt_w_out,
                             filt_freq, filter_dft, batch)
    for l in range(depth):
        hns = [h.reshape(rows, d_model) for h in hns]
        x0c, uv = _hy_proj(hns[0], w_in, l, conv_w[l], conv_b[l], batch)
        y_hy = _hyena(x0c, uv, hy_skip[l], fold_dft, filters, l, batch)
        qkvs = [_qkv_proj(hns[g], w_in, l, g, rope[g], batch) for g in range(N_GROUPS)]
        y_att = _attention(qkvs, batch)
        x2d = _merge(x2d, hns[0], y_hy, y_att, bf(w_in[l, :, HY_IN_WIDTH + QKV_WIDTH:]), bf(p_hy[l]), bf(p_att[l]),
                     bf(w_o[l]), rows)
        last = l == depth - 1
        outs = _ffn(x2d, norm_ffn[l], bf(w_ff1[l]), bf(w_ff2[l]), norm_final if last else norm_mix[l + 1], batch, last)
        x2d, hns = outs[0], outs[1:]
    return x2d.reshape(batch, seq, d_model)
```

```python
import functools
import math

import numpy as np
import jax
import jax.numpy as jnp
from jax import lax
from jax.experimental import pallas as pl
from jax.experimental.pallas import tpu as pltpu

F32 = jnp.float32
BF16 = jnp.bfloat16

D_MODEL = 1024
SEQ = 2048
HEAD_DIM = 64
ATTN_HEADS = 8
DILATIONS = (1, 4, 16)
RADIUS = 64
N_GROUPS = len(DILATIONS)
ATTN_WIDTH = ATTN_HEADS * HEAD_DIM
ROPE_THETA = 10000.0
NEG_INF = -1e30
HYENA_WIDTH = D_MODEL // 2
FILTER_BANDS = 16
FILTER_EMB_DIM = 1 + 2 * FILTER_BANDS
FILTER_HIDDEN = 64
FILTER_INNER = 2
DECAY_TARGET = 1e-2
FAST_DECAY_PCT = 0.3
SLOW_DECAY_PCT = 1.5
HY_IN_WIDTH = 3 * HYENA_WIDTH
GROUP_WIDTH = 3 * ATTN_WIDTH
QKV_WIDTH = N_GROUPS * GROUP_WIDTH
D_FF = 4 * D_MODEL
RMS_EPS = 1e-6

LANES = 128
SUBLANES = 8
N_SLABS = D_MODEL // LANES
N_FFT = 2 * SEQ
HALF = SEQ // 2
FREQ_BLOCK = 512
N_FREQ_BLOCKS = HALF // FREQ_BLOCK
N_FILTER_SETS = 5
FILTER_CHUNK = 128
MLP_ROWS = 256
PHASE_STEPS = SEQ // MLP_ROWS
assert PHASE_STEPS == HALF // FILTER_CHUNK and FREQ_BLOCK % FILTER_CHUNK == 0
Q_BLOCK = 128
K_WINDOW = 256
PAIRS_PER_STEP = 2
ROW_TILE = 512
ROW_CHUNK = 512
VMEM_LIMIT = 56 * 1024 * 1024

assert DILATIONS[0] == 1 and DILATIONS[2] == DILATIONS[1] ** 2


def _params(n_axes, vmem=VMEM_LIMIT):
    return pltpu.CompilerParams(dimension_semantics=("arbitrary",) * n_axes, vmem_limit_bytes=vmem)


def _perm_positions(dilation):
    n = SEQ // dilation
    r = np.arange(SEQ) // n
    m = np.arange(SEQ) % n
    return m * dilation + r


@functools.lru_cache(maxsize=None)
def _rope_tables():
    half = HEAD_DIM // 2
    inv_freq = ROPE_THETA ** (-np.arange(half, dtype=np.float64) / half)
    lane = np.arange(LANES)
    first_half = (lane % HEAD_DIM) < half
    tables = []
    for d in DILATIONS:
        ang = _perm_positions(d)[:, None].astype(np.float64) * inv_freq[None, :]
        cos_l = np.cos(ang)[:, lane % half]
        sin_l = np.sin(ang)[:, lane % half]
        sa = np.where(first_half[None, :], -sin_l, 0.0)
        sb = np.where(first_half[None, :], 0.0, sin_l)
        tables.append(np.stack([cos_l, sa, sb]).astype(np.float32))
    return tuple(tables)


def _blocked(rows_a, rows_b, axis):
    parts = []
    for b in range(N_FREQ_BLOCKS):
        sl = [slice(None)] * rows_a.ndim
        sl[axis] = slice(b * FREQ_BLOCK, (b + 1) * FREQ_BLOCK)
        parts += [rows_a[tuple(sl)], rows_b[tuple(sl)]]
    return np.concatenate(parts, axis=axis)


@functools.lru_cache(maxsize=None)
def _fold_dft_tables():
    j = np.arange(HALF, dtype=np.int64)[:, None]
    n = np.arange(HALF, dtype=np.int64)[None, :]
    ang = lambda k: ((k * (2 * n + 1)) % (2 * N_FFT)).astype(np.float64) * (2.0 * math.pi / (2 * N_FFT))
    ce, se, co, so = np.cos(ang(2 * j)), np.sin(ang(2 * j)), np.cos(ang(2 * j + 1)), np.sin(ang(2 * j + 1))
    se[0, :] = (-1.0) ** np.arange(HALF)
    w_e = np.full((HALF, 1), 2.0 / N_FFT)
    w_e[0] = 1.0 / N_FFT
    w_o = 2.0 / N_FFT
    fwd_e = _blocked(ce, so, 0)
    fwd_o = _blocked(co, se, 0)
    inv_s = _blocked((w_e * ce).T, (-w_o * so).T, 1)
    inv_a = _blocked((-w_e * se).T, (w_o * co).T, 1)
    rev = np.eye(HALF)[::-1]
    return tuple(np.ascontiguousarray(t).astype(np.float32) for t in (fwd_e, fwd_o, inv_s, inv_a, rev))


@functools.lru_cache(maxsize=None)
def _filter_dft_table():
    j = np.arange(HALF, dtype=np.int64)[:, None]
    lag = np.arange(SEQ, dtype=np.int64)[None, :]
    ang = lambda k: ((k * lag) % N_FFT).astype(np.float64) * (2.0 * math.pi / N_FFT)
    ce, se, co, so = np.cos(ang(2 * j)), np.sin(ang(2 * j)), np.cos(ang(2 * j + 1)), np.sin(ang(2 * j + 1))
    se[0, :] = (-1.0) ** np.arange(SEQ)
    parts = []
    for c in range(HALF // FILTER_CHUNK):
        rows = slice(c * FILTER_CHUNK, (c + 1) * FILTER_CHUNK)
        parts += [ce[rows], se[rows], co[rows], so[rows]]
    return np.concatenate(parts, axis=0).astype(np.float32)


@functools.lru_cache(maxsize=None)
def _filter_tables():
    n = np.arange(SEQ, dtype=np.float64)
    t = n / max(SEQ - 1, 1)
    bands = np.linspace(1e-4, FILTER_BANDS - 1, FILTER_BANDS)
    ang = (2.0 * math.pi / SEQ) * n[:, None] * bands[None, :]
    z = np.concatenate([t[:, None], np.cos(ang), -np.sin(ang)], axis=-1)
    z = np.pad(z, ((0, 0), (0, LANES - FILTER_EMB_DIM)))
    max_decay = math.log(DECAY_TARGET) / FAST_DECAY_PCT
    min_decay = math.log(DECAY_TARGET) / SLOW_DECAY_PCT
    deltas = np.abs(np.linspace(min_decay, max_decay, HYENA_WIDTH))
    window = np.exp(-t[:, None] * deltas[None, :])
    return z.astype(np.float32), window.astype(np.float32)


@functools.lru_cache(maxsize=None)
def _attention_masks():
    i = np.arange(Q_BLOCK)[:, None]
    j = np.arange(K_WINDOW)[None, :]
    classes, ids = {}, []
    for d in DILATIONS:
        n = SEQ // d
        for c in range(SEQ // Q_BLOCK):
            q0 = c * Q_BLOCK
            ws = min(max(q0 - RADIUS, 0), SEQ - K_WINDOW)
            seg = (q0 // n) * n
            kpos = ws + j
            valid = (np.abs(kpos - (q0 + i)) <= RADIUS) & (kpos >= seg) & (kpos < seg + n)
            bias = np.where(valid, 0.0, NEG_INF).astype(np.float32)
            ids.append(classes.setdefault(bias.tobytes(), len(classes)))
    table = np.stack([np.frombuffer(b, np.float32).reshape(Q_BLOCK, K_WINDOW) for b in classes])
    return table, np.asarray(ids, np.int32)


def _rms(x, gain):
    return x * lax.rsqrt(jnp.mean(x * x, axis=-1, keepdims=True) + RMS_EPS) * gain


def _store_row_orders(hn, o0_ref, o1_ref, o2_ref, a_ref, b_ref):
    d1, d2 = DILATIONS[1], DILATIONS[2]
    n1, n2 = hn.shape[0] // d1, hn.shape[0] // d2
    o0_ref[...] = hn.astype(BF16)
    for s in range(N_SLABS):
        a_ref[s] = hn[:, s * LANES:(s + 1) * LANES]
    for r in range(d1):
        v = jnp.concatenate([a_ref[s, pl.ds(r, n1, stride=d1), :] for s in range(N_SLABS)], axis=1)
        o1_ref[r] = v.astype(BF16)
        for s in range(N_SLABS):
            b_ref[s, r * n1:(r + 1) * n1, :] = v[:, s * LANES:(s + 1) * LANES]
    for r in range(d2):
        src = pl.ds((r % d1) * n1 + r // d1, n2, stride=d1)
        v = jnp.concatenate([b_ref[s, src, :] for s in range(N_SLABS)], axis=1)
        o2_ref[r] = v.astype(BF16)


def _row_order_outputs(batch, tile):
    d1, d2 = DILATIONS[1], DILATIONS[2]
    tiles = SEQ // tile
    shapes = [jax.ShapeDtypeStruct((batch * SEQ, D_MODEL), BF16),
              jax.ShapeDtypeStruct((batch, d1, SEQ // d1, D_MODEL), BF16),
              jax.ShapeDtypeStruct((batch, d2, SEQ // d2, D_MODEL), BF16)]
    specs = [pl.BlockSpec((tile, D_MODEL), lambda i: (i, 0)),
             pl.BlockSpec((None, d1, tile // d1, D_MODEL), lambda i: (i // tiles, 0, i % tiles, 0)),
             pl.BlockSpec((None, d2, tile // d2, D_MODEL), lambda i: (i // tiles, 0, i % tiles, 0))]
    scratch = [pltpu.VMEM((N_SLABS, tile, LANES), F32), pltpu.VMEM((N_SLABS, tile, LANES), F32)]
    return shapes, specs, scratch


def _filter_step(i, depth):
    phase = i // PHASE_STEPS
    sub = i % PHASE_STEPS
    active = phase < 2 * depth
    layer = jnp.minimum(phase // 2, depth - 1)
    is_mlp = active & (phase % 2 == 0)
    is_spec = active & (phase % 2 == 1)
    chunk = jnp.where(is_spec, sub, jnp.where(active, 0, PHASE_STEPS - 1))
    return layer, is_mlp, is_spec, sub, chunk


def _prologue_kernel(x_ref, g_ref, z_ref, win_ref, w1_ref, b1_ref, wi_ref, bi_ref, wo_ref, fr_ref, kf_ref,
                     o0_ref, o1_ref, o2_ref, h_ref, a_ref, b_ref, hcat_ref, *, depth, n_tiles):
    i = pl.program_id(0)
    hi = lax.Precision.HIGHEST
    width = HYENA_WIDTH

    @pl.when(i < n_tiles)
    def _():
        _store_row_orders(_rms(x_ref[...], g_ref[...]), o0_ref, o1_ref, o2_ref, a_ref, b_ref)

    layer, is_mlp, is_spec, sub, _ = _filter_step(i, depth)

    @pl.when(is_mlp)
    def _():
        half_rows = MLP_ROWS // 2
        r0 = pl.multiple_of(sub * MLP_ROWS, MLP_ROWS)
        halves = (pl.ds(r0, half_rows), pl.ds(r0 + half_rows, half_rows))
        fr = fr_ref[layer]
        zz = jnp.concatenate([z_ref[halves[0], :], z_ref[halves[1], :]], axis=1)
        hid = jnp.sin(fr * (jnp.dot(zz, w1_ref[layer], precision=hi, preferred_element_type=F32) + b1_ref[layer]))
        for l in range(FILTER_INNER):
            hid = jnp.sin(fr * (jnp.dot(hid, wi_ref[layer, l], precision=hi,
                                        preferred_element_type=F32) + bi_ref[layer, l]))
        hid = hid.astype(BF16)
        for h, rows in enumerate(halves):
            filt = jnp.dot(hid, wo_ref[layer, h].astype(BF16), preferred_element_type=F32)
            win = win_ref[rows, :]
            row = r0 + h * half_rows + lax.broadcasted_iota(jnp.int32, (half_rows, 1), 0)
            h_fwd = filt[:, :width] * win
            h_bwd = jnp.where(row == 0, 0.0, filt[:, width:] * win)
            hcat_ref[layer, rows, :width] = h_fwd.astype(BF16)
            hcat_ref[layer, rows, width:] = h_bwd.astype(BF16)

    @pl.when(is_spec)
    def _():
        sp = jnp.dot(kf_ref[...], hcat_ref[layer], preferred_element_type=F32)
        f, b = sp[:, :width], sp[:, width:]
        sets = [slice(g * FILTER_CHUNK, (g + 1) * FILTER_CHUNK) for g in range(4)]
        first_row = (lax.broadcasted_iota(jnp.int32, (FILTER_CHUNK, 1), 0) == 0) & (sub == 0)
        re_even = f[sets[0]] + b[sets[0]]
        nyquist = f[sets[1]][0:1] + b[sets[1]][0:1]
        h_ref[0] = re_even
        h_ref[1] = jnp.where(first_row, 0.0, b[sets[1]] - f[sets[1]])
        h_ref[2] = jnp.where(first_row, nyquist, re_even)
        h_ref[3] = f[sets[2]] + b[sets[2]]
        h_ref[4] = b[sets[3]] - f[sets[3]]


def _prologue(x2d, gain, z, window, w1, b1, w_inner, b_inner, w_out, freq, filter_dft, batch):
    depth = w1.shape[0]
    tile = 2 * ROW_TILE
    n_tiles = batch * SEQ // tile
    steps = max(n_tiles, 2 * depth * PHASE_STEPS)
    resident = lambda a: pl.BlockSpec(a.shape, lambda i: (0,) * a.ndim, pipeline_mode=pl.Buffered(1))
    assert 2 * FILTER_HIDDEN == LANES
    zero = jnp.zeros_like
    diag2 = lambda a: jnp.concatenate([jnp.concatenate([a, zero(a)], axis=-1),
                                       jnp.concatenate([zero(a), a], axis=-1)], axis=-2)
    twice = lambda a: jnp.concatenate([a, a], axis=-1)
    w1p = diag2(jnp.pad(w1, ((0, 0), (0, LANES - FILTER_EMB_DIM), (0, 0))))
    b1 = twice(b1.reshape(depth, 1, FILTER_HIDDEN))
    w_inner = diag2(w_inner)
    b_inner = twice(b_inner.reshape(depth, FILTER_INNER, 1, FILTER_HIDDEN))
    freq = twice(freq.reshape(depth, 1, FILTER_HIDDEN))
    w_out = jnp.stack([jnp.concatenate([w_out, zero(w_out)], axis=1),
                       jnp.concatenate([zero(w_out), w_out], axis=1)], axis=1)
    consts = (z, window, w1p, b1, w_inner, b_inner, w_out, freq)
    shapes, specs, scratch = _row_order_outputs(batch, tile)
    tile_of = lambda i: jnp.minimum(i, n_tiles - 1)
    specs = [pl.BlockSpec(s.block_shape, functools.partial(lambda i, m: m(tile_of(i)), m=s.index_map)) for s in specs]
    per_block = FREQ_BLOCK // FILTER_CHUNK

    def filter_window(i):
        layer, _, _, _, chunk = _filter_step(i, depth)
        return layer, chunk // per_block, 0, chunk % per_block, 0

    outs = pl.pallas_call(
        functools.partial(_prologue_kernel, depth=depth, n_tiles=n_tiles),
        grid=(steps,),
        in_specs=[pl.BlockSpec((tile, D_MODEL), lambda i: (tile_of(i), 0)),
                  pl.BlockSpec((1, D_MODEL), lambda i: (0, 0))] + [resident(a) for a in consts]
                 + [pl.BlockSpec((4 * FILTER_CHUNK, SEQ), lambda i: (_filter_step(i, depth)[4], 0))],
        out_specs=specs + [pl.BlockSpec((None, None, N_FILTER_SETS, FILTER_CHUNK, HYENA_WIDTH), filter_window)],
        out_shape=shapes + [jax.ShapeDtypeStruct((depth, N_FREQ_BLOCKS, N_FILTER_SETS, FREQ_BLOCK, HYENA_WIDTH), F32)],
        scratch_shapes=scratch + [pltpu.VMEM((depth, SEQ, 2 * HYENA_WIDTH), BF16)],
        compiler_params=_params(1),
        name="prologue",
    )(x2d, gain.reshape(1, D_MODEL), *consts, filter_dft)
    return outs[:3], outs[3]


def _hy_proj_kernel(hn_ref, w_ref, cw_ref, cb_ref, x0_ref, uv_ref, wb_ref):
    _cast_weights_once(w_ref, wb_ref)
    n_chunks = SEQ // ROW_CHUNK
    pre = [jnp.dot(hn_ref[rc * ROW_CHUNK:(rc + 1) * ROW_CHUNK, :], wb_ref[...], preferred_element_type=F32)
           for rc in range(n_chunks)]
    w = cw_ref[...]
    bias = cb_ref[...]
    zeros = jnp.zeros((SUBLANES, HY_IN_WIDTH), F32)
    for rc in range(n_chunks):
        before = pre[rc - 1][ROW_CHUNK - SUBLANES:] if rc > 0 else zeros
        after = pre[rc + 1][:SUBLANES] if rc < n_chunks - 1 else zeros
        ext = ROW_CHUNK + SUBLANES
        prev = pltpu.roll(jnp.concatenate([before, pre[rc]], axis=0), 1, 0)[SUBLANES:]
        nxt = pltpu.roll(jnp.concatenate([pre[rc], after], axis=0), ext - 1, 0)[:ROW_CHUNK]
        u = bias + prev * w[0:1] + pre[rc] * w[1:2] + nxt * w[2:3]
        rows = slice(rc * ROW_CHUNK, (rc + 1) * ROW_CHUNK)
        x0_ref[rows, :] = u[:, :HYENA_WIDTH].astype(BF16)
        uv_ref[rows, :] = (u[:, HYENA_WIDTH:2 * HYENA_WIDTH] * u[:, 2 * HYENA_WIDTH:]).astype(BF16)


def _hy_proj(hn0, w_in, layer, conv_w, conv_b, batch):
    const = lambda a: pl.BlockSpec(a.shape, lambda b: (0,) * a.ndim)
    conv_b = conv_b.reshape(1, HY_IN_WIDTH)
    out_spec = pl.BlockSpec((SEQ, HYENA_WIDTH), lambda b: (b, 0))
    out_shape = jax.ShapeDtypeStruct((batch * SEQ, HYENA_WIDTH), BF16)
    return pl.pallas_call(
        _hy_proj_kernel,
        grid=(batch,),
        in_specs=[pl.BlockSpec((SEQ, D_MODEL), lambda b: (b, 0)), _weight_block(layer, HY_IN_WIDTH, 0),
                  const(conv_w), const(conv_b)],
        out_specs=[out_spec, out_spec],
        out_shape=[out_shape, out_shape],
        scratch_shapes=[pltpu.VMEM((D_MODEL, HY_IN_WIDTH), BF16)],
        compiler_params=_params(1),
        name="hy_proj",
    )(hn0, w_in, conv_w, conv_b)


def _cast_weights_once(w_ref, wb_ref):
    @pl.when(pl.program_id(0) == 0)
    def _():
        wb_ref[...] = w_ref[...].astype(BF16)


def _weight_block(layer, width, block):
    return pl.BlockSpec((None, D_MODEL, width), lambda b: (layer, 0, block), pipeline_mode=pl.Buffered(1))


def _qkv_proj_kernel(hn_ref, w_ref, rope_ref, o_ref, wb_ref):
    _cast_weights_once(w_ref, wb_ref)
    half = HEAD_DIM // 2
    q_scale = HEAD_DIM ** -0.5 * math.log2(math.e)
    for rc in range(SEQ // ROW_CHUNK):
        rows = slice(rc * ROW_CHUNK, (rc + 1) * ROW_CHUNK)
        acc = jnp.dot(hn_ref[rows, :], wb_ref[...], preferred_element_type=F32)
        cos, sa, sb = rope_ref[0, rows, :], rope_ref[1, rows, :], rope_ref[2, rows, :]
        for c in range(2 * ATTN_WIDTH // LANES):
            t = acc[:, c * LANES:(c + 1) * LANES]
            if c < ATTN_WIDTH // LANES:
                t = t * q_scale
            out = t * cos + pltpu.roll(t, LANES - half, 1) * sa + pltpu.roll(t, half, 1) * sb
            o_ref[rows, c * LANES:(c + 1) * LANES] = out.astype(BF16)
        o_ref[rows, 2 * ATTN_WIDTH:] = acc[:, 2 * ATTN_WIDTH:].astype(BF16)


def _qkv_proj(hn, w_in, layer, group, rope, batch):
    const = lambda a: pl.BlockSpec(a.shape, lambda b: (0,) * a.ndim)
    assert HY_IN_WIDTH == GROUP_WIDTH
    return pl.pallas_call(
        _qkv_proj_kernel,
        grid=(batch,),
        in_specs=[pl.BlockSpec((SEQ, D_MODEL), lambda b: (b, 0)), _weight_block(layer, GROUP_WIDTH, 1 + group),
                  const(rope)],
        out_specs=pl.BlockSpec((SEQ, GROUP_WIDTH), lambda b: (b, 0)),
        out_shape=jax.ShapeDtypeStruct((batch * SEQ, GROUP_WIDTH), BF16),
        scratch_shapes=[pltpu.VMEM((D_MODEL, GROUP_WIDTH), BF16)],
        compiler_params=_params(1),
        name="qkv_proj",
    )(hn, w_in, rope)


def _hyena_kernel(x0_ref, uv_ref, skip_ref, rev_ref, fe_ref, fo_ref, gs_ref, ga_ref, h_ref, o_ref):
    lo = uv_ref[:HALF, :].astype(F32)
    mirrored = jnp.dot(rev_ref[...], uv_ref[HALF:, :], preferred_element_type=F32)
    e = (lo + mirrored).astype(BF16)
    d = (lo - mirrored).astype(BF16)
    sym = anti = None
    for blk in range(N_FREQ_BLOCKS):
        span = slice(blk * 2 * FREQ_BLOCK, (blk + 1) * 2 * FREQ_BLOCK)
        from_e = jnp.dot(fe_ref[span, :], e, preferred_element_type=F32)
        from_d = jnp.dot(fo_ref[span, :], d, preferred_element_type=F32)
        a_e, b_o = from_e[:FREQ_BLOCK], from_e[FREQ_BLOCK:]
        a_o, b_e = from_d[:FREQ_BLOCK], from_d[FREQ_BLOCK:]
        yr_e = a_e * h_ref[blk, 0] + b_e * h_ref[blk, 1]
        yi_e = a_e * h_ref[blk, 1] - b_e * h_ref[blk, 2]
        yr_o = a_o * h_ref[blk, 3] + b_o * h_ref[blk, 4]
        yi_o = a_o * h_ref[blk, 4] - b_o * h_ref[blk, 3]
        part_s = jnp.dot(gs_ref[:, span], jnp.concatenate([yr_e, yi_o], axis=0).astype(BF16),
                         preferred_element_type=F32)
        part_a = jnp.dot(ga_ref[:, span], jnp.concatenate([yi_e, yr_o], axis=0).astype(BF16),
                         preferred_element_type=F32)
        sym = part_s if sym is None else sym + part_s
        anti = part_a if anti is None else anti + part_a
    skip = skip_ref[...]
    y_lo = sym + anti + skip * lo
    y_hi = (jnp.dot(rev_ref[...], (sym - anti).astype(BF16), preferred_element_type=F32)
            + skip * uv_ref[HALF:, :].astype(F32))
    o_ref[:HALF, :] = (x0_ref[:HALF, :].astype(F32) * y_lo).astype(BF16)
    o_ref[HALF:, :] = (x0_ref[HALF:, :].astype(F32) * y_hi).astype(BF16)


def _hyena(x0c, uv, skip, fold_dft, filters, layer, batch):
    slab = pl.BlockSpec((SEQ, HYENA_WIDTH), lambda b: (b, 0))
    resident = lambda a: pl.BlockSpec(a.shape, lambda b: (0,) * a.ndim, pipeline_mode=pl.Buffered(1))
    fwd_e, fwd_o, inv_s, inv_a, rev = fold_dft
    tables = (rev, fwd_e, fwd_o, inv_s, inv_a, filters)
    layer_filters = pl.BlockSpec((None,) + filters.shape[1:], lambda b: (layer, 0, 0, 0, 0),
                                 pipeline_mode=pl.Buffered(1))
    return pl.pallas_call(
        _hyena_kernel,
        grid=(batch,),
        in_specs=[slab, slab, pl.BlockSpec((1, HYENA_WIDTH), lambda b: (0, 0))]
                 + [resident(t) for t in tables[:-1]] + [layer_filters],
        out_specs=slab,
        out_shape=jax.ShapeDtypeStruct((batch * SEQ, HYENA_WIDTH), BF16),
        compiler_params=_params(1),
        name="hyena",
    )(x0c, uv, skip.reshape(1, HYENA_WIDTH), *tables)


def _attention_kernel(cls_ref, bias_ref, *refs):
    qkv_refs = refs[:3 * N_GROUPS]
    o_ref, acc_ref, den_ref, max_ref = refs[3 * N_GROUPS:]
    for pair in range(PAIRS_PER_STEP):
        lanes = pl.ds(pair * LANES, LANES)
        _attention_pair(cls_ref, bias_ref, [r.at[:, lanes] for r in qkv_refs], o_ref.at[:, lanes],
                        acc_ref.at[pair], den_ref.at[pair], max_ref.at[pair])


def _attention_pair(cls_ref, bias_ref, qkv_refs, o_ref, acc_ref, den_ref, max_ref):
    lane = lax.broadcasted_iota(jnp.int32, (1, LANES), 1)
    first = lane < HEAD_DIM
    ones = jnp.ones((K_WINDOW, LANES), BF16)
    n_blocks = SEQ // Q_BLOCK

    for g, d in reversed(list(enumerate(DILATIONS))):
        q_ref, k_ref, v_ref = qkv_refs[3 * g:3 * g + 3]
        n = SEQ // d

        def q_block(c, carry, g=g, d=d, n=n, q_ref=q_ref, k_ref=k_ref, v_ref=v_ref):
            q0 = pl.multiple_of(c * Q_BLOCK, Q_BLOCK)
            ws = pl.multiple_of(jnp.clip(q0 - RADIUS, 0, SEQ - K_WINDOW), RADIUS)
            q = q_ref[pl.ds(q0, Q_BLOCK), :]
            zero = jnp.zeros_like(q)
            q2 = jnp.concatenate([jnp.where(first, q, zero), jnp.where(first, zero, q)], axis=0)
            s = lax.dot_general(q2, k_ref[pl.ds(ws, K_WINDOW), :], (((1,), (1,)), ((), ())),
                                preferred_element_type=F32)
            bias = bias_ref[cls_ref[g * n_blocks + c]]
            s = s + jnp.concatenate([bias, bias], axis=0)
            m = jnp.max(s, axis=1, keepdims=True)
            p = jnp.exp2(s - m).astype(BF16)
            out = jnp.dot(p, jnp.concatenate([v_ref[pl.ds(ws, K_WINDOW), :], ones], axis=1),
                          preferred_element_type=F32)
            acc = jnp.where(first, out[:Q_BLOCK, :LANES], out[Q_BLOCK:, :LANES])
            den = jnp.where(first, out[:Q_BLOCK, LANES:], out[Q_BLOCK:, LANES:])
            top = jnp.where(first, m[:Q_BLOCK], m[Q_BLOCK:]) + jnp.zeros((Q_BLOCK, LANES), F32)
            if d > 1:
                dst = pl.ds(q0 // n + d * lax.rem(q0, n), Q_BLOCK, stride=d)
                acc_ref[g - 1, dst, :] = acc
                den_ref[g - 1, dst, :] = den
                max_ref[g - 1, dst, :] = top
            else:
                rows = pl.ds(q0, Q_BLOCK)
                ms = [top] + [max_ref[o, rows, :] for o in range(N_GROUPS - 1)]
                accs = [acc] + [acc_ref[o, rows, :] for o in range(N_GROUPS - 1)]
                dens = [den] + [den_ref[o, rows, :] for o in range(N_GROUPS - 1)]
                peak = functools.reduce(jnp.maximum, ms)
                wts = [jnp.exp2(mm - peak) for mm in ms]
                num = sum(w * a for w, a in zip(wts, accs))
                tot = sum(w * dd for w, dd in zip(wts, dens))
                o_ref[rows, :] = (num / tot).astype(BF16)
            return carry

        lax.fori_loop(0, n_blocks, q_block, 0, unroll=n_blocks)


def _attention(qkvs, batch):
    width = PAIRS_PER_STEP * LANES
    steps = ATTN_WIDTH // width
    table, ids = (jnp.asarray(a) for a in _attention_masks())
    specs = [pl.BlockSpec(memory_space=pltpu.SMEM),
             pl.BlockSpec(table.shape, lambda b, hp: (0, 0, 0))]
    operands = []
    for g in range(N_GROUPS):
        for t in range(3):
            specs.append(pl.BlockSpec((SEQ, width), functools.partial(lambda b, hp, t: (b, t * steps + hp), t=t)))
            operands.append(qkvs[g])
    group_rows = pltpu.VMEM((PAIRS_PER_STEP, N_GROUPS - 1, SEQ, LANES), F32)
    return pl.pallas_call(
        _attention_kernel,
        grid=(batch, steps),
        in_specs=specs,
        out_specs=pl.BlockSpec((SEQ, width), lambda b, hp: (b, hp)),
        out_shape=jax.ShapeDtypeStruct((batch * SEQ, ATTN_WIDTH), BF16),
        scratch_shapes=[group_rows, group_rows, group_rows],
        compiler_params=_params(2),
        name="attention",
    )(ids, table, *operands)


def _merge_kernel(x_ref, hn_ref, yh_ref, ya_ref, wg_ref, ph_ref, pa_ref, wo_ref, o_ref):
    hn = hn_ref[...]
    g_hy = jax.nn.sigmoid(jnp.dot(hn, wg_ref[:, :D_MODEL], preferred_element_type=F32))
    merged = g_hy * jnp.dot(yh_ref[...], ph_ref[...], preferred_element_type=F32)
    g_att = jax.nn.sigmoid(jnp.dot(hn, wg_ref[:, D_MODEL:], preferred_element_type=F32))
    merged = merged + g_att * jnp.dot(ya_ref[...], pa_ref[...], preferred_element_type=F32)
    o_ref[...] = x_ref[...] + jnp.dot(merged.astype(BF16), wo_ref[...], preferred_element_type=F32)


def _merge(x2d, hn0, y_hy, y_att, w_gate, p_hy, p_att, w_o, rows):
    tm = 1024
    row = lambda w: pl.BlockSpec((tm, w), lambda i: (i, 0))
    const = lambda a: pl.BlockSpec(a.shape, lambda i: (0, 0))
    return pl.pallas_call(
        _merge_kernel,
        grid=(rows // tm,),
        in_specs=[row(D_MODEL), row(D_MODEL), row(HYENA_WIDTH), row(ATTN_WIDTH),
                  const(w_gate), const(p_hy), const(p_att), const(w_o)],
        out_specs=row(D_MODEL),
        out_shape=jax.ShapeDtypeStruct((rows, D_MODEL), F32),
        compiler_params=_params(1),
        name="merge",
    )(x2d, hn0, y_hy, y_att, w_gate, p_hy, p_att, w_o)


def _ffn_kernel(x_ref, g_ref, w1_ref, w2_ref, gn_ref, *rest, last):
    x = x_ref[...]
    hn = _rms(x, g_ref[...]).astype(BF16)
    chunk = 1024
    y = x
    for c in range(D_FF // chunk):
        h = jnp.dot(hn, w1_ref[:, c * chunk:(c + 1) * chunk], preferred_element_type=F32)
        h = jnp.square(jnp.maximum(h, 0.0)).astype(BF16)
        y = y + jnp.dot(h, w2_ref[c * chunk:(c + 1) * chunk, :], preferred_element_type=F32)
    if last:
        (o_ref,) = rest
        o_ref[...] = _rms(y, gn_ref[...])
    else:
        o_ref, o0_ref, o1_ref, o2_ref, a_ref, b_ref = rest
        o_ref[...] = y
        _store_row_orders(_rms(y, gn_ref[...]), o0_ref, o1_ref, o2_ref, a_ref, b_ref)


def _ffn(x2d, gain, w1, w2, gain_next, batch, last):
    rows = batch * SEQ
    tile = 2 * ROW_TILE if last else ROW_TILE
    row = pl.BlockSpec((tile, D_MODEL), lambda i: (i, 0))
    const = lambda a: pl.BlockSpec(a.shape, lambda i: (0, 0))
    gain = gain.reshape(1, D_MODEL)
    gain_next = gain_next.reshape(1, D_MODEL)
    out_shape = [jax.ShapeDtypeStruct((rows, D_MODEL), F32)]
    out_specs = [row]
    scratch = []
    if not last:
        shapes, specs, scratch = _row_order_outputs(batch, tile)
        out_shape += shapes
        out_specs += specs
    return pl.pallas_call(
        functools.partial(_ffn_kernel, last=last),
        grid=(rows // tile,),
        in_specs=[row, const(gain), const(w1), const(w2), const(gain_next)],
        out_specs=out_specs,
        out_shape=out_shape,
        scratch_shapes=scratch,
        compiler_params=_params(1),
        name="ffn",
    )(x2d, gain, w1, w2, gain_next)


def kernel(x, norm_mix, w_in, conv_w, conv_b, filt_w1, filt_b1, filt_w_inner, filt_b_inner, filt_w_out,
           filt_freq, hy_skip, p_hy, p_att, w_o, norm_ffn, w_ff1, w_ff2, norm_final):
    batch, seq, d_model = x.shape
    assert (seq, d_model) == (SEQ, D_MODEL)
    depth = norm_mix.shape[0]
    rows = batch * seq
    bf = lambda a: a.astype(BF16)
    rope = tuple(jnp.asarray(t) for t in _rope_tables())
    fold_dft = tuple(bf(jnp.asarray(t)) for t in _fold_dft_tables())
    filter_dft = bf(jnp.asarray(_filter_dft_table()))
    z, window = (jnp.asarray(t) for t in _filter_tables())

    x2d = x.reshape(rows, d_model)
    hns, filters = _prologue(x2d, norm_mix[0], z, window, filt_w1, filt_b1, filt_w_inner, filt_b_inner, filt_w_out,
                             filt_freq, filter_dft, batch)
    for l in range(depth):
        hns = [h.reshape(rows, d_model) for h in hns]
        x0c, uv = _hy_proj(hns[0], w_in, l, conv_w[l], conv_b[l], batch)
        y_hy = _hyena(x0c, uv, hy_skip[l], fold_dft, filters, l, batch)
        qkvs = [_qkv_proj(hns[g], w_in, l, g, rope[g], batch) for g in range(N_GROUPS)]
        y_att = _attention(qkvs, batch)
        x2d = _merge(x2d, hns[0], y_hy, y_att, bf(w_in[l, :, HY_IN_WIDTH + QKV_WIDTH:]), bf(p_hy[l]), bf(p_att[l]),
                     bf(w_o[l]), rows)
        last = l == depth - 1
        outs = _ffn(x2d, norm_ffn[l], bf(w_ff1[l]), bf(w_ff2[l]), norm_final if last else norm_mix[l + 1], batch, last)
        x2d, hns = outs[0], outs[1:]
    return x2d.reshape(batch, seq, d_model)
```

```python
import functools
import math

import numpy as np
import jax
import jax.numpy as jnp
from jax import lax
from jax.experimental import pallas as pl
from jax.experimental.pallas import tpu as pltpu

F32 = jnp.float32
BF16 = jnp.bfloat16

D_MODEL = 1024
SEQ = 2048
HEAD_DIM = 64
ATTN_HEADS = 8
DILATIONS = (1, 4, 16)
RADIUS = 64
N_GROUPS = len(DILATIONS)
ATTN_WIDTH = ATTN_HEADS * HEAD_DIM
ROPE_THETA = 10000.0
NEG_INF = -1e30
HYENA_WIDTH = D_MODEL // 2
FILTER_BANDS = 16
FILTER_EMB_DIM = 1 + 2 * FILTER_BANDS
FILTER_HIDDEN = 64
FILTER_INNER = 2
DECAY_TARGET = 1e-2
FAST_DECAY_PCT = 0.3
SLOW_DECAY_PCT = 1.5
HY_IN_WIDTH = 3 * HYENA_WIDTH
GROUP_WIDTH = 3 * ATTN_WIDTH
QKV_WIDTH = N_GROUPS * GROUP_WIDTH
D_FF = 4 * D_MODEL
RMS_EPS = 1e-6

LANES = 128
SUBLANES = 8
N_SLABS = D_MODEL // LANES
N_FFT = 2 * SEQ
HALF = SEQ // 2
FREQ_BLOCK = 512
N_FREQ_BLOCKS = HALF // FREQ_BLOCK
N_FILTER_SETS = 5
FILTER_CHUNK = 128
MLP_ROWS = 256
PHASE_STEPS = SEQ // MLP_ROWS
assert PHASE_STEPS == HALF // FILTER_CHUNK and FREQ_BLOCK % FILTER_CHUNK == 0
Q_BLOCK = 128
K_WINDOW = 256
PAIRS_PER_STEP = 2
ROW_TILE = 512
ROW_CHUNK = 512
VMEM_LIMIT = 56 * 1024 * 1024

assert DILATIONS[0] == 1 and DILATIONS[2] == DILATIONS[1] ** 2


def _params(n_axes, vmem=VMEM_LIMIT, independent=False):
    semantics = ("parallel" if independent else "arbitrary",) * n_axes
    return pltpu.CompilerParams(dimension_semantics=semantics, vmem_limit_bytes=vmem)


def _perm_positions(dilation):
    n = SEQ // dilation
    r = np.arange(SEQ) // n
    m = np.arange(SEQ) % n
    return m * dilation + r


@functools.lru_cache(maxsize=None)
def _rope_tables():
    half = HEAD_DIM // 2
    inv_freq = ROPE_THETA ** (-np.arange(half, dtype=np.float64) / half)
    lane = np.arange(LANES)
    first_half = (lane % HEAD_DIM) < half
    tables = []
    for d in DILATIONS:
        ang = _perm_positions(d)[:, None].astype(np.float64) * inv_freq[None, :]
        cos_l = np.cos(ang)[:, lane % half]
        sin_l = np.sin(ang)[:, lane % half]
        sa = np.where(first_half[None, :], -sin_l, 0.0)
        sb = np.where(first_half[None, :], 0.0, sin_l)
        tables.append(np.stack([cos_l, sa, sb]).astype(np.float32))
    return tuple(tables)


def _blocked(rows_a, rows_b, axis):
    parts = []
    for b in range(N_FREQ_BLOCKS):
        sl = [slice(None)] * rows_a.ndim
        sl[axis] = slice(b * FREQ_BLOCK, (b + 1) * FREQ_BLOCK)
        parts += [rows_a[tuple(sl)], rows_b[tuple(sl)]]
    return np.concatenate(parts, axis=axis)


@functools.lru_cache(maxsize=None)
def _fold_dft_tables():
    j = np.arange(HALF, dtype=np.int64)[:, None]
    n = np.arange(HALF, dtype=np.int64)[None, :]
    ang = lambda k: ((k * (2 * n + 1)) % (2 * N_FFT)).astype(np.float64) * (2.0 * math.pi / (2 * N_FFT))
    ce, se, co, so = np.cos(ang(2 * j)), np.sin(ang(2 * j)), np.cos(ang(2 * j + 1)), np.sin(ang(2 * j + 1))
    se[0, :] = (-1.0) ** np.arange(HALF)
    w_e = np.full((HALF, 1), 2.0 / N_FFT)
    w_e[0] = 1.0 / N_FFT
    w_o = 2.0 / N_FFT
    fwd_e = _blocked(ce, so, 0)
    fwd_o = _blocked(co, se, 0)
    inv_s = _blocked((w_e * ce).T, (-w_o * so).T, 1)
    inv_a = _blocked((-w_e * se).T, (w_o * co).T, 1)
    rev = np.eye(HALF)[::-1]
    return tuple(np.ascontiguousarray(t).astype(np.float32) for t in (fwd_e, fwd_o, inv_s, inv_a, rev))


@functools.lru_cache(maxsize=None)
def _filter_dft_table():
    j = np.arange(HALF, dtype=np.int64)[:, None]
    lag = np.arange(SEQ, dtype=np.int64)[None, :]
    ang = lambda k: ((k * lag) % N_FFT).astype(np.float64) * (2.0 * math.pi / N_FFT)
    ce, se, co, so = np.cos(ang(2 * j)), np.sin(ang(2 * j)), np.cos(ang(2 * j + 1)), np.sin(ang(2 * j + 1))
    se[0, :] = (-1.0) ** np.arange(SEQ)
    parts = []
    for c in range(HALF // FILTER_CHUNK):
        rows = slice(c * FILTER_CHUNK, (c + 1) * FILTER_CHUNK)
        parts += [ce[rows], se[rows], co[rows], so[rows]]
    return np.concatenate(parts, axis=0).astype(np.float32)


@functools.lru_cache(maxsize=None)
def _filter_tables():
    n = np.arange(SEQ, dtype=np.float64)
    t = n / max(SEQ - 1, 1)
    bands = np.linspace(1e-4, FILTER_BANDS - 1, FILTER_BANDS)
    ang = (2.0 * math.pi / SEQ) * n[:, None] * bands[None, :]
    z = np.concatenate([t[:, None], np.cos(ang), -np.sin(ang)], axis=-1)
    z = np.pad(z, ((0, 0), (0, LANES - FILTER_EMB_DIM)))
    max_decay = math.log(DECAY_TARGET) / FAST_DECAY_PCT
    min_decay = math.log(DECAY_TARGET) / SLOW_DECAY_PCT
    deltas = np.abs(np.linspace(min_decay, max_decay, HYENA_WIDTH))
    window = np.exp(-t[:, None] * deltas[None, :])
    return z.astype(np.float32), window.astype(np.float32)


@functools.lru_cache(maxsize=None)
def _attention_masks():
    i = np.arange(Q_BLOCK)[:, None]
    j = np.arange(K_WINDOW)[None, :]
    classes, ids = {}, []
    for d in DILATIONS:
        n = SEQ // d
        for c in range(SEQ // Q_BLOCK):
            q0 = c * Q_BLOCK
            ws = min(max(q0 - RADIUS, 0), SEQ - K_WINDOW)
            seg = (q0 // n) * n
            kpos = ws + j
            valid = (np.abs(kpos - (q0 + i)) <= RADIUS) & (kpos >= seg) & (kpos < seg + n)
            bias = np.where(valid, 0.0, NEG_INF).astype(np.float32)
            ids.append(classes.setdefault(bias.tobytes(), len(classes)))
    table = np.stack([np.frombuffer(b, np.float32).reshape(Q_BLOCK, K_WINDOW) for b in classes])
    return table, np.asarray(ids, np.int32)


def _rms(x, gain):
    return x * lax.rsqrt(jnp.mean(x * x, axis=-1, keepdims=True) + RMS_EPS) * gain


def _store_row_orders(hn, o0_ref, o1_ref, o2_ref, a_ref, b_ref):
    d1, d2 = DILATIONS[1], DILATIONS[2]
    n1, n2 = hn.shape[0] // d1, hn.shape[0] // d2
    o0_ref[...] = hn.astype(BF16)
    for s in range(N_SLABS):
        a_ref[s] = hn[:, s * LANES:(s + 1) * LANES]
    for r in range(d1):
        v = jnp.concatenate([a_ref[s, pl.ds(r, n1, stride=d1), :] for s in range(N_SLABS)], axis=1)
        o1_ref[r] = v.astype(BF16)
        for s in range(N_SLABS):
            b_ref[s, r * n1:(r + 1) * n1, :] = v[:, s * LANES:(s + 1) * LANES]
    for r in range(d2):
        src = pl.ds((r % d1) * n1 + r // d1, n2, stride=d1)
        v = jnp.concatenate([b_ref[s, src, :] for s in range(N_SLABS)], axis=1)
        o2_ref[r] = v.astype(BF16)


def _row_order_outputs(batch, tile):
    d1, d2 = DILATIONS[1], DILATIONS[2]
    tiles = SEQ // tile
    shapes = [jax.ShapeDtypeStruct((batch * SEQ, D_MODEL), BF16),
              jax.ShapeDtypeStruct((batch, d1, SEQ // d1, D_MODEL), BF16),
              jax.ShapeDtypeStruct((batch, d2, SEQ // d2, D_MODEL), BF16)]
    specs = [pl.BlockSpec((tile, D_MODEL), lambda i: (i, 0)),
             pl.BlockSpec((None, d1, tile // d1, D_MODEL), lambda i: (i // tiles, 0, i % tiles, 0)),
             pl.BlockSpec((None, d2, tile // d2, D_MODEL), lambda i: (i // tiles, 0, i % tiles, 0))]
    scratch = [pltpu.VMEM((N_SLABS, tile, LANES), F32), pltpu.VMEM((N_SLABS, tile, LANES), F32)]
    return shapes, specs, scratch


def _filter_step(i, depth):
    phase = i // PHASE_STEPS
    sub = i % PHASE_STEPS
    active = phase < 2 * depth
    layer = jnp.minimum(phase // 2, depth - 1)
    is_mlp = active & (phase % 2 == 0)
    is_spec = active & (phase % 2 == 1)
    chunk = jnp.where(is_spec, sub, jnp.where(active, 0, PHASE_STEPS - 1))
    return layer, is_mlp, is_spec, sub, chunk


def _prologue_kernel(x_ref, g_ref, z_ref, win_ref, w1_ref, b1_ref, wi_ref, bi_ref, wo_ref, fr_ref, kf_ref,
                     o0_ref, o1_ref, o2_ref, h_ref, a_ref, b_ref, hcat_ref, *, depth, n_tiles):
    i = pl.program_id(0)
    hi = lax.Precision.HIGHEST
    width = HYENA_WIDTH

    @pl.when(i < n_tiles)
    def _():
        _store_row_orders(_rms(x_ref[...], g_ref[...]), o0_ref, o1_ref, o2_ref, a_ref, b_ref)

    layer, is_mlp, is_spec, sub, _ = _filter_step(i, depth)

    @pl.when(is_mlp)
    def _():
        half_rows = MLP_ROWS // 2
        r0 = pl.multiple_of(sub * MLP_ROWS, MLP_ROWS)
        halves = (pl.ds(r0, half_rows), pl.ds(r0 + half_rows, half_rows))
        fr = fr_ref[layer]
        zz = jnp.concatenate([z_ref[halves[0], :], z_ref[halves[1], :]], axis=1)
        hid = jnp.sin(fr * (jnp.dot(zz, w1_ref[layer], precision=hi, preferred_element_type=F32) + b1_ref[layer]))
        for l in range(FILTER_INNER):
            hid = jnp.sin(fr * (jnp.dot(hid, wi_ref[layer, l], precision=hi,
                                        preferred_element_type=F32) + bi_ref[layer, l]))
        hid = hid.astype(BF16)
        for h, rows in enumerate(halves):
            filt = jnp.dot(hid, wo_ref[layer, h].astype(BF16), preferred_element_type=F32)
            win = win_ref[rows, :]
            row = r0 + h * half_rows + lax.broadcasted_iota(jnp.int32, (half_rows, 1), 0)
            h_fwd = filt[:, :width] * win
            h_bwd = jnp.where(row == 0, 0.0, filt[:, width:] * win)
            hcat_ref[layer, rows, :width] = h_fwd.astype(BF16)
            hcat_ref[layer, rows, width:] = h_bwd.astype(BF16)

    @pl.when(is_spec)
    def _():
        sp = jnp.dot(kf_ref[...], hcat_ref[layer], preferred_element_type=F32)
        f, b = sp[:, :width], sp[:, width:]
        sets = [slice(g * FILTER_CHUNK, (g + 1) * FILTER_CHUNK) for g in range(4)]
        first_row = (lax.broadcasted_iota(jnp.int32, (FILTER_CHUNK, 1), 0) == 0) & (sub == 0)
        re_even = f[sets[0]] + b[sets[0]]
        nyquist = f[sets[1]][0:1] + b[sets[1]][0:1]
        h_ref[0] = re_even
        h_ref[1] = jnp.where(first_row, 0.0, b[sets[1]] - f[sets[1]])
        h_ref[2] = jnp.where(first_row, nyquist, re_even)
        h_ref[3] = f[sets[2]] + b[sets[2]]
        h_ref[4] = b[sets[3]] - f[sets[3]]


def _prologue(x2d, gain, z, window, w1, b1, w_inner, b_inner, w_out, freq, filter_dft, batch):
    depth = w1.shape[0]
    tile = 2 * ROW_TILE
    n_tiles = batch * SEQ // tile
    steps = max(n_tiles, 2 * depth * PHASE_STEPS)
    resident = lambda a: pl.BlockSpec(a.shape, lambda i: (0,) * a.ndim, pipeline_mode=pl.Buffered(1))
    assert 2 * FILTER_HIDDEN == LANES
    zero = jnp.zeros_like
    diag2 = lambda a: jnp.concatenate([jnp.concatenate([a, zero(a)], axis=-1),
                                       jnp.concatenate([zero(a), a], axis=-1)], axis=-2)
    twice = lambda a: jnp.concatenate([a, a], axis=-1)
    w1p = diag2(jnp.pad(w1, ((0, 0), (0, LANES - FILTER_EMB_DIM), (0, 0))))
    b1 = twice(b1.reshape(depth, 1, FILTER_HIDDEN))
    w_inner = diag2(w_inner)
    b_inner = twice(b_inner.reshape(depth, FILTER_INNER, 1, FILTER_HIDDEN))
    freq = twice(freq.reshape(depth, 1, FILTER_HIDDEN))
    w_out = jnp.stack([jnp.concatenate([w_out, zero(w_out)], axis=1),
                       jnp.concatenate([zero(w_out), w_out], axis=1)], axis=1)
    consts = (z, window, w1p, b1, w_inner, b_inner, w_out, freq)
    shapes, specs, scratch = _row_order_outputs(batch, tile)
    tile_of = lambda i: jnp.minimum(i, n_tiles - 1)
    specs = [pl.BlockSpec(s.block_shape, functools.partial(lambda i, m: m(tile_of(i)), m=s.index_map)) for s in specs]
    per_block = FREQ_BLOCK // FILTER_CHUNK

    def filter_window(i):
        layer, _, _, _, chunk = _filter_step(i, depth)
        return layer, chunk // per_block, 0, chunk % per_block, 0

    outs = pl.pallas_call(
        functools.partial(_prologue_kernel, depth=depth, n_tiles=n_tiles),
        grid=(steps,),
        in_specs=[pl.BlockSpec((tile, D_MODEL), lambda i: (tile_of(i), 0)),
                  pl.BlockSpec((1, D_MODEL), lambda i: (0, 0))] + [resident(a) for a in consts]
                 + [pl.BlockSpec((4 * FILTER_CHUNK, SEQ), lambda i: (_filter_step(i, depth)[4], 0))],
        out_specs=specs + [pl.BlockSpec((None, None, N_FILTER_SETS, FILTER_CHUNK, HYENA_WIDTH), filter_window)],
        out_shape=shapes + [jax.ShapeDtypeStruct((depth, N_FREQ_BLOCKS, N_FILTER_SETS, FREQ_BLOCK, HYENA_WIDTH), F32)],
        scratch_shapes=scratch + [pltpu.VMEM((depth, SEQ, 2 * HYENA_WIDTH), BF16)],
        compiler_params=_params(1),
        name="prologue",
    )(x2d, gain.reshape(1, D_MODEL), *consts, filter_dft)
    return outs[:3], outs[3]


def _hy_proj_kernel(hn_ref, w_ref, cw_ref, cb_ref, x0_ref, uv_ref, wb_ref):
    _cast_weights_once(w_ref, wb_ref)
    n_chunks = SEQ // ROW_CHUNK
    pre = [jnp.dot(hn_ref[rc * ROW_CHUNK:(rc + 1) * ROW_CHUNK, :], wb_ref[...], preferred_element_type=F32)
           for rc in range(n_chunks)]
    w = cw_ref[...]
    bias = cb_ref[...]
    zeros = jnp.zeros((SUBLANES, HY_IN_WIDTH), F32)
    for rc in range(n_chunks):
        before = pre[rc - 1][ROW_CHUNK - SUBLANES:] if rc > 0 else zeros
        after = pre[rc + 1][:SUBLANES] if rc < n_chunks - 1 else zeros
        ext = ROW_CHUNK + SUBLANES
        prev = pltpu.roll(jnp.concatenate([before, pre[rc]], axis=0), 1, 0)[SUBLANES:]
        nxt = pltpu.roll(jnp.concatenate([pre[rc], after], axis=0), ext - 1, 0)[:ROW_CHUNK]
        u = bias + prev * w[0:1] + pre[rc] * w[1:2] + nxt * w[2:3]
        rows = slice(rc * ROW_CHUNK, (rc + 1) * ROW_CHUNK)
        x0_ref[rows, :] = u[:, :HYENA_WIDTH].astype(BF16)
        uv_ref[rows, :] = (u[:, HYENA_WIDTH:2 * HYENA_WIDTH] * u[:, 2 * HYENA_WIDTH:]).astype(BF16)


def _hy_proj(hn0, w_in, layer, conv_w, conv_b, batch):
    const = lambda a: pl.BlockSpec(a.shape, lambda b: (0,) * a.ndim)
    conv_b = conv_b.reshape(1, HY_IN_WIDTH)
    out_spec = pl.BlockSpec((SEQ, HYENA_WIDTH), lambda b: (b, 0))
    out_shape = jax.ShapeDtypeStruct((batch * SEQ, HYENA_WIDTH), BF16)
    return pl.pallas_call(
        _hy_proj_kernel,
        grid=(batch,),
        in_specs=[pl.BlockSpec((SEQ, D_MODEL), lambda b: (b, 0)), _weight_block(layer, HY_IN_WIDTH, 0),
                  const(conv_w), const(conv_b)],
        out_specs=[out_spec, out_spec],
        out_shape=[out_shape, out_shape],
        scratch_shapes=[pltpu.VMEM((D_MODEL, HY_IN_WIDTH), BF16)],
        compiler_params=_params(1),
        name="hy_proj",
    )(hn0, w_in, conv_w, conv_b)


def _cast_weights_once(w_ref, wb_ref):
    @pl.when(pl.program_id(0) == 0)
    def _():
        wb_ref[...] = w_ref[...].astype(BF16)


def _weight_block(layer, width, block):
    return pl.BlockSpec((None, D_MODEL, width), lambda b: (layer, 0, block), pipeline_mode=pl.Buffered(1))


def _qkv_proj_kernel(hn_ref, w_ref, rope_ref, o_ref, wb_ref):
    _cast_weights_once(w_ref, wb_ref)
    half = HEAD_DIM // 2
    q_scale = HEAD_DIM ** -0.5 * math.log2(math.e)
    for rc in range(SEQ // ROW_CHUNK):
        rows = slice(rc * ROW_CHUNK, (rc + 1) * ROW_CHUNK)
        acc = jnp.dot(hn_ref[rows, :], wb_ref[...], preferred_element_type=F32)
        cos, sa, sb = rope_ref[0, rows, :], rope_ref[1, rows, :], rope_ref[2, rows, :]
        for c in range(2 * ATTN_WIDTH // LANES):
            t = acc[:, c * LANES:(c + 1) * LANES]
            if c < ATTN_WIDTH // LANES:
                t = t * q_scale
            out = t * cos + pltpu.roll(t, LANES - half, 1) * sa + pltpu.roll(t, half, 1) * sb
            o_ref[rows, c * LANES:(c + 1) * LANES] = out.astype(BF16)
        o_ref[rows, 2 * ATTN_WIDTH:] = acc[:, 2 * ATTN_WIDTH:].astype(BF16)


def _qkv_proj(hn, w_in, layer, group, rope, batch):
    const = lambda a: pl.BlockSpec(a.shape, lambda b: (0,) * a.ndim)
    assert HY_IN_WIDTH == GROUP_WIDTH
    return pl.pallas_call(
        _qkv_proj_kernel,
        grid=(batch,),
        in_specs=[pl.BlockSpec((SEQ, D_MODEL), lambda b: (b, 0)), _weight_block(layer, GROUP_WIDTH, 1 + group),
                  const(rope)],
        out_specs=pl.BlockSpec((SEQ, GROUP_WIDTH), lambda b: (b, 0)),
        out_shape=jax.ShapeDtypeStruct((batch * SEQ, GROUP_WIDTH), BF16),
        scratch_shapes=[pltpu.VMEM((D_MODEL, GROUP_WIDTH), BF16)],
        compiler_params=_params(1),
        name="qkv_proj",
    )(hn, w_in, rope)


def _hyena_kernel(x0_ref, uv_ref, skip_ref, rev_ref, fe_ref, fo_ref, gs_ref, ga_ref, h_ref, o_ref):
    lo = uv_ref[:HALF, :].astype(F32)
    mirrored = jnp.dot(rev_ref[...], uv_ref[HALF:, :], preferred_element_type=F32)
    e = (lo + mirrored).astype(BF16)
    d = (lo - mirrored).astype(BF16)
    sym = anti = None
    for blk in range(N_FREQ_BLOCKS):
        span = slice(blk * 2 * FREQ_BLOCK, (blk + 1) * 2 * FREQ_BLOCK)
        from_e = jnp.dot(fe_ref[span, :], e, preferred_element_type=F32)
        from_d = jnp.dot(fo_ref[span, :], d, preferred_element_type=F32)
        a_e, b_o = from_e[:FREQ_BLOCK], from_e[FREQ_BLOCK:]
        a_o, b_e = from_d[:FREQ_BLOCK], from_d[FREQ_BLOCK:]
        yr_e = a_e * h_ref[blk, 0] + b_e * h_ref[blk, 1]
        yi_e = a_e * h_ref[blk, 1] - b_e * h_ref[blk, 2]
        yr_o = a_o * h_ref[blk, 3] + b_o * h_ref[blk, 4]
        yi_o = a_o * h_ref[blk, 4] - b_o * h_ref[blk, 3]
        part_s = jnp.dot(gs_ref[:, span], jnp.concatenate([yr_e, yi_o], axis=0).astype(BF16),
                         preferred_element_type=F32)
        part_a = jnp.dot(ga_ref[:, span], jnp.concatenate([yi_e, yr_o], axis=0).astype(BF16),
                         preferred_element_type=F32)
        sym = part_s if sym is None else sym + part_s
        anti = part_a if anti is None else anti + part_a
    skip = skip_ref[...]
    y_lo = sym + anti + skip * lo
    y_hi = (jnp.dot(rev_ref[...], (sym - anti).astype(BF16), preferred_element_type=F32)
            + skip * uv_ref[HALF:, :].astype(F32))
    o_ref[:HALF, :] = (x0_ref[:HALF, :].astype(F32) * y_lo).astype(BF16)
    o_ref[HALF:, :] = (x0_ref[HALF:, :].astype(F32) * y_hi).astype(BF16)


def _hyena(x0c, uv, skip, fold_dft, filters, layer, batch):
    slab = pl.BlockSpec((SEQ, HYENA_WIDTH), lambda b: (b, 0))
    resident = lambda a: pl.BlockSpec(a.shape, lambda b: (0,) * a.ndim, pipeline_mode=pl.Buffered(1))
    fwd_e, fwd_o, inv_s, inv_a, rev = fold_dft
    tables = (rev, fwd_e, fwd_o, inv_s, inv_a, filters)
    layer_filters = pl.BlockSpec((None,) + filters.shape[1:], lambda b: (layer, 0, 0, 0, 0),
                                 pipeline_mode=pl.Buffered(1))
    return pl.pallas_call(
        _hyena_kernel,
        grid=(batch,),
        in_specs=[slab, slab, pl.BlockSpec((1, HYENA_WIDTH), lambda b: (0, 0))]
                 + [resident(t) for t in tables[:-1]] + [layer_filters],
        out_specs=slab,
        out_shape=jax.ShapeDtypeStruct((batch * SEQ, HYENA_WIDTH), BF16),
        compiler_params=_params(1, independent=True),
        name="hyena",
    )(x0c, uv, skip.reshape(1, HYENA_WIDTH), *tables)


def _attention_kernel(cls_ref, bias_ref, *refs):
    qkv_refs = refs[:3 * N_GROUPS]
    o_ref, acc_ref, den_ref, max_ref = refs[3 * N_GROUPS:]
    for pair in range(PAIRS_PER_STEP):
        lanes = pl.ds(pair * LANES, LANES)
        _attention_pair(cls_ref, bias_ref, [r.at[:, lanes] for r in qkv_refs], o_ref.at[:, lanes],
                        acc_ref.at[pair], den_ref.at[pair], max_ref.at[pair])


def _attention_pair(cls_ref, bias_ref, qkv_refs, o_ref, acc_ref, den_ref, max_ref):
    lane = lax.broadcasted_iota(jnp.int32, (1, LANES), 1)
    first = lane < HEAD_DIM
    ones = jnp.ones((K_WINDOW, LANES), BF16)
    n_blocks = SEQ // Q_BLOCK

    for g, d in reversed(list(enumerate(DILATIONS))):
        q_ref, k_ref, v_ref = qkv_refs[3 * g:3 * g + 3]
        n = SEQ // d

        def q_block(c, carry, g=g, d=d, n=n, q_ref=q_ref, k_ref=k_ref, v_ref=v_ref):
            q0 = pl.multiple_of(c * Q_BLOCK, Q_BLOCK)
            ws = pl.multiple_of(jnp.clip(q0 - RADIUS, 0, SEQ - K_WINDOW), RADIUS)
            q = q_ref[pl.ds(q0, Q_BLOCK), :]
            zero = jnp.zeros_like(q)
            q2 = jnp.concatenate([jnp.where(first, q, zero), jnp.where(first, zero, q)], axis=0)
            s = lax.dot_general(q2, k_ref[pl.ds(ws, K_WINDOW), :], (((1,), (1,)), ((), ())),
                                preferred_element_type=F32)
            bias = bias_ref[cls_ref[g * n_blocks + c]]
            s = s + jnp.concatenate([bias, bias], axis=0)
            m = jnp.max(s, axis=1, keepdims=True)
            p = jnp.exp2(s - m).astype(BF16)
            out = jnp.dot(p, jnp.concatenate([v_ref[pl.ds(ws, K_WINDOW), :], ones], axis=1),
                          preferred_element_type=F32)
            acc = jnp.where(first, out[:Q_BLOCK, :LANES], out[Q_BLOCK:, :LANES])
            den = jnp.where(first, out[:Q_BLOCK, LANES:], out[Q_BLOCK:, LANES:])
            top = jnp.where(first, m[:Q_BLOCK], m[Q_BLOCK:]) + jnp.zeros((Q_BLOCK, LANES), F32)
            if d > 1:
                dst = pl.ds(q0 // n + d * lax.rem(q0, n), Q_BLOCK, stride=d)
                acc_ref[g - 1, dst, :] = acc
                den_ref[g - 1, dst, :] = den
                max_ref[g - 1, dst, :] = top
            else:
                rows = pl.ds(q0, Q_BLOCK)
                ms = [top] + [max_ref[o, rows, :] for o in range(N_GROUPS - 1)]
                accs = [acc] + [acc_ref[o, rows, :] for o in range(N_GROUPS - 1)]
                dens = [den] + [den_ref[o, rows, :] for o in range(N_GROUPS - 1)]
                peak = functools.reduce(jnp.maximum, ms)
                wts = [jnp.exp2(mm - peak) for mm in ms]
                num = sum(w * a for w, a in zip(wts, accs))
                tot = sum(w * dd for w, dd in zip(wts, dens))
                o_ref[rows, :] = (num / tot).astype(BF16)
            return carry

        lax.fori_loop(0, n_blocks, q_block, 0, unroll=n_blocks)


def _attention(qkvs, batch):
    width = PAIRS_PER_STEP * LANES
    steps = ATTN_WIDTH // width
    table, ids = (jnp.asarray(a) for a in _attention_masks())
    specs = [pl.BlockSpec(memory_space=pltpu.SMEM),
             pl.BlockSpec(table.shape, lambda b, hp: (0, 0, 0))]
    operands = []
    for g in range(N_GROUPS):
        for t in range(3):
            specs.append(pl.BlockSpec((SEQ, width), functools.partial(lambda b, hp, t: (b, t * steps + hp), t=t)))
            operands.append(qkvs[g])
    group_rows = pltpu.VMEM((PAIRS_PER_STEP, N_GROUPS - 1, SEQ, LANES), F32)
    return pl.pallas_call(
        _attention_kernel,
        grid=(batch, steps),
        in_specs=specs,
        out_specs=pl.BlockSpec((SEQ, width), lambda b, hp: (b, hp)),
        out_shape=jax.ShapeDtypeStruct((batch * SEQ, ATTN_WIDTH), BF16),
        scratch_shapes=[group_rows, group_rows, group_rows],
        compiler_params=_params(2, independent=True),
        name="attention",
    )(ids, table, *operands)


def _merge_kernel(x_ref, hn_ref, yh_ref, ya_ref, wg_ref, ph_ref, pa_ref, wo_ref, o_ref):
    hn = hn_ref[...]
    g_hy = jax.nn.sigmoid(jnp.dot(hn, wg_ref[:, :D_MODEL], preferred_element_type=F32))
    merged = g_hy * jnp.dot(yh_ref[...], ph_ref[...], preferred_element_type=F32)
    g_att = jax.nn.sigmoid(jnp.dot(hn, wg_ref[:, D_MODEL:], preferred_element_type=F32))
    merged = merged + g_att * jnp.dot(ya_ref[...], pa_ref[...], preferred_element_type=F32)
    o_ref[...] = x_ref[...] + jnp.dot(merged.astype(BF16), wo_ref[...], preferred_element_type=F32)


def _merge(x2d, hn0, y_hy, y_att, w_gate, p_hy, p_att, w_o, rows):
    tm = 1024
    row = lambda w: pl.BlockSpec((tm, w), lambda i: (i, 0))
    const = lambda a: pl.BlockSpec(a.shape, lambda i: (0, 0))
    return pl.pallas_call(
        _merge_kernel,
        grid=(rows // tm,),
        in_specs=[row(D_MODEL), row(D_MODEL), row(HYENA_WIDTH), row(ATTN_WIDTH),
                  const(w_gate), const(p_hy), const(p_att), const(w_o)],
        out_specs=row(D_MODEL),
        out_shape=jax.ShapeDtypeStruct((rows, D_MODEL), F32),
        compiler_params=_params(1, independent=True),
        name="merge",
    )(x2d, hn0, y_hy, y_att, w_gate, p_hy, p_att, w_o)


def _ffn_kernel(x_ref, g_ref, w1_ref, w2_ref, gn_ref, *rest, last):
    x = x_ref[...]
    hn = _rms(x, g_ref[...]).astype(BF16)
    chunk = 1024
    y = x
    for c in range(D_FF // chunk):
        h = jnp.dot(hn, w1_ref[:, c * chunk:(c + 1) * chunk], preferred_element_type=F32)
        h = jnp.square(jnp.maximum(h, 0.0)).astype(BF16)
        y = y + jnp.dot(h, w2_ref[c * chunk:(c + 1) * chunk, :], preferred_element_type=F32)
    if last:
        (o_ref,) = rest
        o_ref[...] = _rms(y, gn_ref[...])
    else:
        o_ref, o0_ref, o1_ref, o2_ref, a_ref, b_ref = rest
        o_ref[...] = y
        _store_row_orders(_rms(y, gn_ref[...]), o0_ref, o1_ref, o2_ref, a_ref, b_ref)


def _ffn(x2d, gain, w1, w2, gain_next, batch, last):
    rows = batch * SEQ
    tile = 2 * ROW_TILE if last else ROW_TILE
    row = pl.BlockSpec((tile, D_MODEL), lambda i: (i, 0))
    const = lambda a: pl.BlockSpec(a.shape, lambda i: (0, 0))
    gain = gain.reshape(1, D_MODEL)
    gain_next = gain_next.reshape(1, D_MODEL)
    out_shape = [jax.ShapeDtypeStruct((rows, D_MODEL), F32)]
    out_specs = [row]
    scratch = []
    if not last:
        shapes, specs, scratch = _row_order_outputs(batch, tile)
        out_shape += shapes
        out_specs += specs
    return pl.pallas_call(
        functools.partial(_ffn_kernel, last=last),
        grid=(rows // tile,),
        in_specs=[row, const(gain), const(w1), const(w2), const(gain_next)],
        out_specs=out_specs,
        out_shape=out_shape,
        scratch_shapes=scratch,
        compiler_params=_params(1, independent=True),
        name="ffn",
    )(x2d, gain, w1, w2, gain_next)


def kernel(x, norm_mix, w_in, conv_w, conv_b, filt_w1, filt_b1, filt_w_inner, filt_b_inner, filt_w_out,
           filt_freq, hy_skip, p_hy, p_att, w_o, norm_ffn, w_ff1, w_ff2, norm_final):
    batch, seq, d_model = x.shape
    assert (seq, d_model) == (SEQ, D_MODEL)
    depth = norm_mix.shape[0]
    rows = batch * seq
    bf = lambda a: a.astype(BF16)
    rope = tuple(jnp.asarray(t) for t in _rope_tables())
    fold_dft = tuple(bf(jnp.asarray(t)) for t in _fold_dft_tables())
    filter_dft = bf(jnp.asarray(_filter_dft_table()))
    z, window = (jnp.asarray(t) for t in _filter_tables())

    x2d = x.reshape(rows, d_model)
    hns, filters = _prologue(x2d, norm_mix[0], z, window, filt_w1, filt_b1, filt_w_inner, filt_b_inner, filt_w_out,
                             filt_freq, filter_dft, batch)
    for l in range(depth):
        hns = [h.reshape(rows, d_model) for h in hns]
        x0c, uv = _hy_proj(hns[0], w_in, l, conv_w[l], conv_b[l], batch)
        y_hy = _hyena(x0c, uv, hy_skip[l], fold_dft, filters, l, batch)
        qkvs = [_qkv_proj(hns[g], w_in, l, g, rope[g], batch) for g in range(N_GROUPS)]
        y_att = _attention(qkvs, batch)
        x2d = _merge(x2d, hns[0], y_hy, y_att, bf(w_in[l, :, HY_IN_WIDTH + QKV_WIDTH:]), bf(p_hy[l]), bf(p_att[l]),
                     bf(w_o[l]), rows)
        last = l == depth - 1
        outs = _ffn(x2d, norm_ffn[l], bf(w_ff1[l]), bf(w_ff2[l]), norm_final if last else norm_mix[l + 1], batch, last)
        x2d, hns = outs[0], outs[1:]
    return x2d.reshape(batch, seq, d_model)
```
